```python
import math
import jax
import jax.numpy as jnp
from jax import lax
import numpy as np

D_MODEL = 1024
BATCH = 8
SEQ = 4096
DEPTH = 4

GRID_W = 64
CTX_LEN = 256
EPS = 1e-6
N_DIR = 2

GDN_HEADS = 4
GDN_HEAD_DIM = 128
GDN_WIDTH = GDN_HEADS * GDN_HEAD_DIM
GDN_CHUNK = 64
GDN_CONV = 3

SSD_HEADS = 8
SSD_HEAD_DIM = 64
SSD_WIDTH = SSD_HEADS * SSD_HEAD_DIM
SSD_GROUPS = 2
SSD_STATE = 128
SSD_CHUNK = 64
SSD_CONV = 3
SSD_CONV_CH = SSD_WIDTH + 2 * SSD_GROUPS * SSD_STATE

ATT_HEADS = 8
ATT_KV_HEADS = 2
ATT_HEAD_DIM = 64
ATT_WIDTH = ATT_HEADS * ATT_HEAD_DIM
ATT_KV_WIDTH = ATT_KV_HEADS * ATT_HEAD_DIM
WINDOW = 128
ATT_BLOCK = 128
ROPE_BASE = 10000.0
ROPE_FREQS = ATT_HEAD_DIM // 4

MLP_GROUPS = 4
MLP_GROUP_DIM = 128
MLP_WIDTH = MLP_GROUPS * MLP_GROUP_DIM
MLP_CHUNK = 128

N_BRANCH = 4
BRANCH_WIDTH = 512

FF_DENSE = 2816
N_EXPERTS = 8
TOP_K = 2
FF_EXPERT = 3584

PROJ_SIZES = (3 * GDN_WIDTH, GDN_WIDTH, N_DIR * GDN_HEADS, N_DIR * GDN_HEADS,
              SSD_CONV_CH, SSD_WIDTH, N_DIR * SSD_HEADS,
              ATT_WIDTH, ATT_KV_WIDTH, ATT_KV_WIDTH,
              MLP_WIDTH, MLP_WIDTH,
              N_BRANCH * D_MODEL)
PROJ_TOTAL = sum(PROJ_SIZES)

kernel_name = 'hybrid_dit_gdn_ssd_swa_gmlp_moe'


def rms_norm(t, gain):
    tf = t.astype(jnp.float32)
    tf = tf * lax.rsqrt(jnp.mean(tf * tf, axis=-1, keepdims=True) + EPS)
    return (tf * gain.astype(jnp.float32)).astype(t.dtype)


def l2_normalize(t):
    tf = t.astype(jnp.float32)
    return (tf * lax.rsqrt(jnp.sum(tf * tf, axis=-1, keepdims=True) + EPS)).astype(t.dtype)


def modulate(h, shift, scale):
    return h * (1.0 + scale) + shift


def split_cols(t, sizes):
    return jnp.split(t, [int(s) for s in np.cumsum(sizes)[:-1]], axis=-1)


def centred_dwconv(t, w):
    k, ch = w.shape
    return lax.conv_general_dilated(t, w[:, None, :].astype(t.dtype), window_strides=(1,),
                                    padding=[(k // 2, k // 2)],
                                    dimension_numbers=('NWC', 'WIO', 'NWC'),
                                    feature_group_count=ch)


def axial_rope(t, cos, sin):
    tt = t.reshape(*t.shape[:-1], 2, 2, ROPE_FREQS)
    t1, t2 = tt[..., 0, :], tt[..., 1, :]
    cc, ss = cos[:, None], sin[:, None]
    out = jnp.stack([t1 * cc - t2 * ss, t2 * cc + t1 * ss], axis=-2)
    return out.reshape(t.shape)


def prefix_scan_bidir(scan_fn, ctx_fw, lat_fw, ctx_bw, lat_bw, state0):
    flip = lambda ts: tuple(jnp.flip(a, axis=1) for a in ts)
    yc_f, sc_f = scan_fn(*ctx_fw, state0)
    yl_f, _ = scan_fn(*lat_fw, sc_f)
    yc_b, sc_b = scan_fn(*flip(ctx_bw), state0)
    yl_b, _ = scan_fn(*flip(lat_bw), sc_b)
    return yc_f + jnp.flip(yc_b, axis=1), yl_f + jnp.flip(yl_b, axis=1)


def gated_delta_chunked(q, k, v, g, beta, state):
    bsz, seq, heads, _ = q.shape
    dv = v.shape[-1]
    n = seq // GDN_CHUNK

    def chunks(t):
        t = t.astype(jnp.float32).reshape(bsz, n, GDN_CHUNK, heads, *t.shape[3:])
        return jnp.moveaxis(t, 3, 2)

    q, k, v, g, beta = (chunks(t) for t in (q, k, v, g, beta))
    gcum = jnp.cumsum(g, axis=-1)
    idx = jnp.arange(GDN_CHUNK)
    incl = idx[:, None] >= idx[None, :]
    decay = jnp.exp(jnp.where(incl, gcum[..., :, None] - gcum[..., None, :], -jnp.inf))
    kb = k * beta[..., None]
    strict = jnp.where(idx[:, None] > idx[None, :],
                       jnp.einsum('bnhid,bnhjd->bnhij', kb, k) * decay, 0.0)
    rhs = jnp.concatenate([v * beta[..., None], kb * jnp.exp(gcum)[..., None]], axis=-1)
    sol = lax.linalg.triangular_solve(strict + jnp.eye(GDN_CHUNK, dtype=jnp.float32), rhs,
                                      left_side=True, lower=True, unit_diagonal=True)
    u, w = sol[..., :dv], sol[..., dv:]
    qk = jnp.einsum('bnhid,bnhjd->bnhij', q, k) * decay
    q_in = q * jnp.exp(gcum)[..., None]
    g_last = gcum[..., -1]
    k_out = k * jnp.exp(g_last[..., None] - gcum)[..., None]

    def step(s, xs):
        u_c, w_c, qk_c, q_c, k_c, gl_c = xs
        v_new = u_c - jnp.einsum('bhcd,bhde->bhce', w_c, s)
        o = jnp.einsum('bhcd,bhde->bhce', q_c, s) + jnp.einsum('bhij,bhje->bhie', qk_c, v_new)
        s = s * jnp.exp(gl_c)[..., None, None] + jnp.einsum('bhcd,bhce->bhde', k_c, v_new)
        return s, o

    xs = tuple(jnp.moveaxis(t, 1, 0) for t in (u, w, qk, q_in, k_out, g_last))
    state, o = lax.scan(step, state, xs)
    o = jnp.transpose(o, (1, 0, 3, 2, 4)).reshape(bsz, seq, heads, dv)
    return o, state


def ssd_chunked(xdt, a, bm, cm, state):
    bsz, seq, heads, hp = xdt.shape
    grp, nst = bm.shape[2:]
    rep = heads // grp
    n = seq // SSD_CHUNK
    x = xdt.astype(jnp.float32).reshape(bsz, n, SSD_CHUNK, grp, rep, hp)
    a = jnp.moveaxis(a.astype(jnp.float32).reshape(bsz, n, SSD_CHUNK, grp, rep), 2, -1)
    bc = bm.astype(jnp.float32).reshape(bsz, n, SSD_CHUNK, grp, nst)
    cc = cm.astype(jnp.float32).reshape(bsz, n, SSD_CHUNK, grp, nst)
    acs = jnp.cumsum(a, axis=-1)
    idx = jnp.arange(SSD_CHUNK)
    incl = idx[:, None] >= idx[None, :]
    lmat = jnp.exp(jnp.where(incl, acs[..., :, None] - acs[..., None, :], -jnp.inf))
    cb = jnp.einsum('bnigm,bnjgm->bngij', cc, bc)
    y_diag = jnp.einsum('bngij,bngrij,bnjgrp->bnigrp', cb, lmat, x)
    decay_out = jnp.exp(acs[..., -1:] - acs)
    chunk_states = jnp.einsum('bnjgm,bngrj,bnjgrp->bngrpm', bc, decay_out, x)
    a_tot = acs[..., -1]

    def step(s, xs):
        st, at = xs
        return s * jnp.exp(at)[..., None, None] + st, s

    state, s_in = lax.scan(step, state, (jnp.moveaxis(chunk_states, 1, 0), jnp.moveaxis(a_tot, 1, 0)))
    s_in = jnp.moveaxis(s_in, 0, 1)
    y_off = jnp.einsum('bnigm,bngrpm,bngri->bnigrp', cc, s_in, jnp.exp(acs))
    return (y_diag + y_off).reshape(bsz, seq, heads, hp), state


def gdn_prep(p_qkv, p_a, p_b, conv_w, a_log, dt_bias):
    bsz, seq = p_qkv.shape[:2]
    qkv = jax.nn.silu(centred_dwconv(p_qkv, conv_w))
    q, k, v = jnp.split(qkv, 3, axis=-1)
    q = l2_normalize(q.reshape(bsz, seq, GDN_HEADS, GDN_HEAD_DIM)) * (GDN_HEAD_DIM ** -0.5)
    k = l2_normalize(k.reshape(bsz, seq, GDN_HEADS, GDN_HEAD_DIM))
    v = v.reshape(bsz, seq, GDN_HEADS, GDN_HEAD_DIM)
    a = p_a.reshape(bsz, seq, N_DIR, GDN_HEADS).astype(jnp.float32)
    g = -jnp.exp(a_log.astype(jnp.float32)) * jax.nn.softplus(a + dt_bias.astype(jnp.float32))
    beta = jax.nn.sigmoid(p_b.reshape(bsz, seq, N_DIR, GDN_HEADS).astype(jnp.float32))
    return q, k, v, g, beta


def gdn_branch(c_qkv, c_gate, c_a, c_b, l_qkv, l_gate, l_a, l_b, conv_w, a_log, dt_bias, norm_g):
    qc, kc, vc, gc, bc = gdn_prep(c_qkv, c_a, c_b, conv_w, a_log, dt_bias)
    ql, kl, vl, gl, bl = gdn_prep(l_qkv, l_a, l_b, conv_w, a_log, dt_bias)
    s0 = jnp.zeros((qc.shape[0], GDN_HEADS, GDN_HEAD_DIM, GDN_HEAD_DIM), jnp.float32)
    oc, ol = prefix_scan_bidir(
        gated_delta_chunked,
        (qc, kc, vc, gc[:, :, 0], bc[:, :, 0]), (ql, kl, vl, gl[:, :, 0], bl[:, :, 0]),
        (qc, kc, vc, gc[:, :, 1], bc[:, :, 1]), (ql, kl, vl, gl[:, :, 1], bl[:, :, 1]), s0)

    def finish(o, gate):
        bsz, seq = gate.shape[:2]
        o = rms_norm(o.astype(gate.dtype), norm_g) * jax.nn.silu(gate.reshape(bsz, seq, GDN_HEADS, GDN_HEAD_DIM))
        return o.reshape(bsz, seq, GDN_WIDTH)

    return finish(oc, c_gate), finish(ol, l_gate)


def ssd_prep(p_xbc, p_dt, conv_w, conv_b, a_log, dt_bias):
    bsz, seq = p_xbc.shape[:2]
    xbc = jax.nn.silu(centred_dwconv(p_xbc, conv_w) + conv_b)
    x, bm, cm = jnp.split(xbc, [SSD_WIDTH, SSD_WIDTH + SSD_GROUPS * SSD_STATE], axis=-1)
    x = x.reshape(bsz, seq, SSD_HEADS, SSD_HEAD_DIM)
    bm = bm.reshape(bsz, seq, SSD_GROUPS, SSD_STATE)
    cm = cm.reshape(bsz, seq, SSD_GROUPS, SSD_STATE)
    dt = jax.nn.softplus(p_dt.reshape(bsz, seq, N_DIR, SSD_HEADS).astype(jnp.float32) + dt_bias.astype(jnp.float32))
    a = -jnp.exp(a_log.astype(jnp.float32)) * dt
    xdt = x.astype(jnp.float32)[:, :, None] * dt[..., None]
    return x, xdt, a, bm, cm


def ssd_branch(c_xbc, c_z, c_dt, l_xbc, l_z, l_dt, conv_w, conv_b, a_log, dt_bias, d_skip, norm_g):
    xc, xdtc, ac, bc, cc = ssd_prep(c_xbc, c_dt, conv_w, conv_b, a_log, dt_bias)
    xl, xdtl, al, bl, cl = ssd_prep(l_xbc, l_dt, conv_w, conv_b, a_log, dt_bias)
    s0 = jnp.zeros((xc.shape[0], SSD_GROUPS, SSD_HEADS // SSD_GROUPS, SSD_HEAD_DIM, SSD_STATE), jnp.float32)
    yc, yl = prefix_scan_bidir(
        ssd_chunked,
        (xdtc[:, :, 0], ac[:, :, 0], bc, cc), (xdtl[:, :, 0], al[:, :, 0], bl, cl),
        (xdtc[:, :, 1], ac[:, :, 1], bc, cc), (xdtl[:, :, 1], al[:, :, 1], bl, cl), s0)

    def finish(y, x, z):
        bsz, seq = z.shape[:2]
        y = y + d_skip.astype(jnp.float32)[:, None] * x.astype(jnp.float32)
        y = y.astype(z.dtype).reshape(bsz, seq, SSD_WIDTH) * jax.nn.silu(z)
        y = rms_norm(y.reshape(bsz, seq, SSD_GROUPS, SSD_WIDTH // SSD_GROUPS),
                     norm_g.reshape(SSD_GROUPS, SSD_WIDTH // SSD_GROUPS))
        return y.reshape(bsz, seq, SSD_WIDTH)

    return finish(yc, xc, c_z), finish(yl, xl, l_z)


def window_attention(q, k, v, kc, vc, sink):
    bsz, seq = q.shape[:2]
    nb = seq // ATT_BLOCK
    grp, rep, blk = ATT_KV_HEADS, ATT_HEADS // ATT_KV_HEADS, ATT_BLOCK
    qb = q.reshape(bsz, nb, blk, grp, rep, ATT_HEAD_DIM) * (ATT_HEAD_DIM ** -0.5)

    def band(t):
        tp = jnp.pad(t, ((0, 0), (blk, blk), (0, 0), (0, 0))).reshape(bsz, nb + 2, blk, grp, ATT_HEAD_DIM)
        return jnp.concatenate([tp[:, :-2], tp[:, 1:-1], tp[:, 2:]], axis=2)

    kb, vb = band(k), band(v)
    qpos = jnp.arange(seq).reshape(nb, blk)
    kpos = (jnp.arange(nb)[:, None] - 1) * blk + jnp.arange(3 * blk)[None, :]
    valid = ((jnp.abs(qpos[:, :, None] - kpos[:, None, :]) <= WINDOW)
             & (kpos >= 0)[:, None, :] & (kpos < seq)[:, None, :])
    s_loc = jnp.einsum('bnqgrd,bnkgd->bngrqk', qb, kb, preferred_element_type=jnp.float32)
    s_loc = jnp.where(valid[None, :, None, None], s_loc, -jnp.inf)
    s_ctx = jnp.einsum('bnqgrd,bkgd->bngrqk', qb, kc, preferred_element_type=jnp.float32)
    s_sink = jnp.broadcast_to(sink.astype(jnp.float32).reshape(1, 1, grp, rep, 1, 1), s_loc.shape[:-1] + (1,))
    p = jax.nn.softmax(jnp.concatenate([s_loc, s_ctx, s_sink], axis=-1), axis=-1).astype(v.dtype)
    nk, lc = 3 * blk, kc.shape[1]
    o = (jnp.einsum('bngrqk,bnkgd->bnqgrd', p[..., :nk], vb)
         + jnp.einsum('bngrqk,bkgd->bnqgrd', p[..., nk:nk + lc], vc))
    return o.reshape(bsz, seq, ATT_WIDTH)


def context_attention(qc, kc, vc, sink):
    bsz, lc = qc.shape[:2]
    grp, rep = ATT_KV_HEADS, ATT_HEADS // ATT_KV_HEADS
    qg = qc.reshape(bsz, lc, grp, rep, ATT_HEAD_DIM) * (ATT_HEAD_DIM ** -0.5)
    s = jnp.einsum('bqgrd,bkgd->bgrqk', qg, kc, preferred_element_type=jnp.float32)
    s_sink = jnp.broadcast_to(sink.astype(jnp.float32).reshape(1, grp, rep, 1, 1), s.shape[:-1] + (1,))
    p = jax.nn.softmax(jnp.concatenate([s, s_sink], axis=-1), axis=-1)[..., :lc].astype(vc.dtype)
    o = jnp.einsum('bgrqk,bkgd->bqgrd', p, vc)
    return o.reshape(bsz, lc, ATT_WIDTH)


def attn_branch(c_q, c_k, c_v, l_q, l_k, l_v, cos, sin, sink, ctx_out):
    heads = lambda t, h: t.reshape(*t.shape[:2], h, ATT_HEAD_DIM)
    kc, vc = heads(c_k, ATT_KV_HEADS), heads(c_v, ATT_KV_HEADS)
    ql = axial_rope(heads(l_q, ATT_HEADS), cos, sin)
    kl = axial_rope(heads(l_k, ATT_KV_HEADS), cos, sin)
    yl = window_attention(ql, kl, heads(l_v, ATT_KV_HEADS), kc, vc, sink)
    yc = context_attention(heads(c_q, ATT_HEADS), kc, vc, sink) if ctx_out else None
    return yc, yl


def chunk_mlp(p_u, p_v, ws, bs):
    bsz, seq = p_u.shape[:2]
    u = jax.nn.gelu(p_u)
    v = jax.nn.gelu(p_v).reshape(bsz, seq // MLP_CHUNK, MLP_CHUNK, MLP_GROUPS, MLP_GROUP_DIM)
    vf = v.astype(jnp.float32)
    mu = jnp.mean(vf, axis=-1, keepdims=True)
    var = jnp.mean(jnp.square(vf - mu), axis=-1, keepdims=True)
    v = ((vf - mu) * lax.rsqrt(var + EPS)).astype(p_u.dtype)
    v = jnp.einsum('gij,bnjgc->bnigc', ws, v) + bs.T[None, None, :, :, None]
    return u * v.reshape(bsz, seq, MLP_WIDTH)


def merge_branches(ys, gate_cols, w_branch, w_out):
    g = gate_cols.reshape(*gate_cols.shape[:-1], N_BRANCH, D_MODEL)
    acc = jax.nn.sigmoid(g[..., 0, :]) * (ys[0] @ w_branch[0])
    for m in range(1, N_BRANCH):
        acc = acc + jax.nn.sigmoid(g[..., m, :]) * (ys[m] @ w_branch[m])
    return acc @ w_out


def token_mixer(hc, hl, cos, sin, w_in, gdn_conv, gdn_A_log, gdn_dt_bias, gdn_norm,
                ssd_conv, ssd_conv_b, ssd_A_log, ssd_dt_bias, ssd_D, ssd_norm,
                attn_sink, mlp_ws, mlp_bs, w_branch, w_out, ctx_out):
    (c_qkv, c_og, c_a, c_b, c_xbc, c_z, c_dt, c_q, c_k, c_v, c_u, c_sv, c_gate) = split_cols(hc @ w_in, PROJ_SIZES)
    (l_qkv, l_og, l_a, l_b, l_xbc, l_z, l_dt, l_q, l_k, l_v, l_u, l_sv, l_gate) = split_cols(hl @ w_in, PROJ_SIZES)
    ya_c, ya_l = gdn_branch(c_qkv, c_og, c_a, c_b, l_qkv, l_og, l_a, l_b,
                            gdn_conv, gdn_A_log, gdn_dt_bias, gdn_norm)
    yb_c, yb_l = ssd_branch(c_xbc, c_z, c_dt, l_xbc, l_z, l_dt,
                            ssd_conv, ssd_conv_b, ssd_A_log, ssd_dt_bias, ssd_D, ssd_norm)
    yc_c, yc_l = attn_branch(c_q, c_k, c_v, l_q, l_k, l_v, cos, sin, attn_sink, ctx_out)
    yd_l = chunk_mlp(l_u, l_sv, mlp_ws, mlp_bs)
    out_l = merge_branches((ya_l, yb_l, yc_l, yd_l), l_gate, w_branch, w_out)
    if not ctx_out:
        return None, out_l
    yd_c = chunk_mlp(c_u, c_sv, mlp_ws, mlp_bs)
    out_c = merge_branches((ya_c, yb_c, yc_c, yd_c), c_gate, w_branch, w_out)
    return out_c, out_l


def swiglu(h, wg, wu, wd):
    return (jax.nn.silu(h @ wg) * (h @ wu)) @ wd


def moe_swiglu(h, router, wg, wu, wd):
    logits = jnp.matmul(h, router, preferred_element_type=jnp.float32)
    top_val, top_idx = lax.top_k(logits, TOP_K)
    top_w = jax.nn.softmax(top_val, axis=-1)
    gate = jnp.sum(jax.nn.one_hot(top_idx, N_EXPERTS, dtype=jnp.float32) * top_w[..., None], axis=-2).astype(h.dtype)
    out = gate[..., 0:1] * swiglu(h, wg[0], wu[0], wd[0])
    for e in range(1, N_EXPERTS):
        out = out + gate[..., e:e + 1] * swiglu(h, wg[e], wu[e], wd[e])
    return out


def channel_mixer(h, layer, ffn_wg, ffn_wu, ffn_wd, moe_router, moe_wg, moe_wu, moe_wd):
    j = layer // 2
    if layer % 2 == 0:
        return swiglu(h, ffn_wg[j], ffn_wu[j], ffn_wd[j])
    return moe_swiglu(h, moe_router[j], moe_wg[j], moe_wu[j], moe_wd[j])


def setup_inputs(seed: int = 0) -> dict:
    key = jax.random.key(seed)
    keys = iter(jax.random.split(key, 64))
    f32 = jnp.float32
    n_dense = (DEPTH + 1) // 2
    n_moe = DEPTH // 2

    def nrm(shape, scale):
        return jax.random.normal(next(keys), shape, f32) * scale

    def gain(shape):
        return 1.0 + nrm(shape, 0.02)

    def a_log(shape):
        return jnp.log(jax.random.uniform(next(keys), shape, f32, 1.0, 16.0))

    def dt_bias(shape):
        dt = jnp.exp(jax.random.uniform(next(keys), shape, f32, math.log(1e-3), math.log(1e-1)))
        return dt + jnp.log(-jnp.expm1(-dt))

    return {
        'x': nrm((BATCH, SEQ, D_MODEL), 1.0),
        'c': nrm((BATCH, D_MODEL), 1.0),
        'ctx': nrm((BATCH, CTX_LEN, D_MODEL), 1.0),
        'c_ctx': nrm((D_MODEL,), 1.0),
        'w_ada': nrm((DEPTH, D_MODEL, 6 * D_MODEL), 0.5 * D_MODEL ** -0.5),
        'b_ada': nrm((DEPTH, 6 * D_MODEL), 0.01),
        'norm1': gain((DEPTH, D_MODEL)),
        'norm2': gain((DEPTH, D_MODEL)),
        'w_in': nrm((DEPTH, D_MODEL, PROJ_TOTAL), D_MODEL ** -0.5),
        'gdn_conv': nrm((DEPTH, GDN_CONV, 3 * GDN_WIDTH), GDN_CONV ** -0.5),
        'gdn_A_log': a_log((DEPTH, N_DIR, GDN_HEADS)),
        'gdn_dt_bias': dt_bias((DEPTH, N_DIR, GDN_HEADS)),
        'gdn_norm': gain((DEPTH, GDN_HEAD_DIM)),
        'ssd_conv': nrm((DEPTH, SSD_CONV, SSD_CONV_CH), SSD_CONV ** -0.5),
        'ssd_conv_b': nrm((DEPTH, SSD_CONV_CH), 0.01),
        'ssd_A_log': a_log((DEPTH, N_DIR, SSD_HEADS)),
        'ssd_dt_bias': dt_bias((DEPTH, N_DIR, SSD_HEADS)),
        'ssd_D': gain((DEPTH, SSD_HEADS)),
        'ssd_norm': gain((DEPTH, SSD_WIDTH)),
        'attn_sink': nrm((DEPTH, ATT_HEADS), 1.0),
        'mlp_ws': nrm((DEPTH, MLP_GROUPS, MLP_CHUNK, MLP_CHUNK), MLP_CHUNK ** -0.5),
        'mlp_bs': 1.0 + nrm((DEPTH, MLP_GROUPS, MLP_CHUNK), 0.01),
        'w_branch': nrm((DEPTH, N_BRANCH, BRANCH_WIDTH, D_MODEL), BRANCH_WIDTH ** -0.5),
        'w_out': nrm((DEPTH, D_MODEL, D_MODEL), D_MODEL ** -0.5),
        'ffn_wg': nrm((n_dense, D_MODEL, FF_DENSE), D_MODEL ** -0.5),
        'ffn_wu': nrm((n_dense, D_MODEL, FF_DENSE), D_MODEL ** -0.5),
        'ffn_wd': nrm((n_dense, FF_DENSE, D_MODEL), FF_DENSE ** -0.5),
        'moe_router': nrm((n_moe, D_MODEL, N_EXPERTS), D_MODEL ** -0.5),
        'moe_wg': nrm((n_moe, N_EXPERTS, D_MODEL, FF_EXPERT), D_MODEL ** -0.5),
        'moe_wu': nrm((n_moe, N_EXPERTS, D_MODEL, FF_EXPERT), D_MODEL ** -0.5),
        'moe_wd': nrm((n_moe, N_EXPERTS, FF_EXPERT, D_MODEL), FF_EXPERT ** -0.5),
        'final_norm': gain((D_MODEL,)),
    }


def reference(x, c, ctx, c_ctx, w_ada, b_ada, norm1, norm2, w_in, gdn_conv, gdn_A_log, gdn_dt_bias,
              gdn_norm, ssd_conv, ssd_conv_b, ssd_A_log, ssd_dt_bias, ssd_D, ssd_norm, attn_sink,
              mlp_ws, mlp_bs, w_branch, w_out, ffn_wg, ffn_wu, ffn_wd, moe_router, moe_wg, moe_wu,
              moe_wd, final_norm):
    bsz, seq = x.shape[:2]
    rows = seq // GRID_W
    row = jnp.repeat(jnp.arange(rows), GRID_W)
    col = jnp.tile(jnp.arange(GRID_W), rows)
    inv_freq = ROPE_BASE ** (-jnp.arange(ROPE_FREQS, dtype=jnp.float32) / ROPE_FREQS)
    ang = jnp.stack([row, col], axis=-1).astype(jnp.float32)[..., None] * inv_freq
    cos = jnp.cos(ang).astype(x.dtype)
    sin = jnp.sin(ang).astype(x.dtype)

    c_act = jax.nn.silu(c)
    cc_act = jax.nn.silu(c_ctx)
    xl, xc = x, ctx
    for i in range(DEPTH):
        last = i == DEPTH - 1
        mod_l = (c_act @ w_ada[i] + b_ada[i]).reshape(bsz, 6, 1, D_MODEL)
        mod_c = (cc_act @ w_ada[i] + b_ada[i]).reshape(6, D_MODEL)
        hl = modulate(rms_norm(xl, norm1[i]), mod_l[:, 0], mod_l[:, 1])
        hc = modulate(rms_norm(xc, norm1[i]), mod_c[0], mod_c[1])
        yc, yl = token_mixer(hc, hl, cos, sin, w_in[i], gdn_conv[i], gdn_A_log[i], gdn_dt_bias[i],
                             gdn_norm[i], ssd_conv[i], ssd_conv_b[i], ssd_A_log[i], ssd_dt_bias[i],
                             ssd_D[i], ssd_norm[i], attn_sink[i], mlp_ws[i], mlp_bs[i],
                             w_branch[i], w_out[i], not last)
        xl = xl + mod_l[:, 2] * yl
        hl2 = modulate(rms_norm(xl, norm2[i]), mod_l[:, 3], mod_l[:, 4])
        xl = xl + mod_l[:, 5] * channel_mixer(hl2, i, ffn_wg, ffn_wu, ffn_wd, moe_router, moe_wg, moe_wu, moe_wd)
        if not last:
            xc = xc + mod_c[2] * yc
            hc2 = modulate(rms_norm(xc, norm2[i]), mod_c[3], mod_c[4])
            xc = xc + mod_c[5] * channel_mixer(hc2, i, ffn_wg, ffn_wu, ffn_wd, moe_router, moe_wg, moe_wu, moe_wd)
    return rms_norm(xl, final_norm)
```

```python
import functools
import math

import jax
import jax.numpy as jnp
import numpy as np
from jax import lax
from jax.experimental import pallas as pl
from jax.experimental.pallas import tpu as pltpu

F32 = jnp.float32
BF16 = jnp.bfloat16

D_MODEL = 1024
GRID_W = 64
EPS = 1e-6
N_DIR = 2

GDN_HEADS = 4
GDN_HEAD_DIM = 128
GDN_WIDTH = 512
SSD_HEADS = 8
SSD_HEAD_DIM = 64
SSD_WIDTH = 512
SSD_GROUPS = 2
SSD_STATE = 128
SSD_CONV_CH = 1024
ATT_HEADS = 8
ATT_KV_HEADS = 2
ATT_HEAD_DIM = 64
ATT_WIDTH = 512
ATT_KV_WIDTH = 128
WINDOW = 128
ATT_BLOCK = 128
ROPE_BASE = 10000.0
ROPE_FREQS = 16
MLP_GROUPS = 4
MLP_GROUP_DIM = 128
MLP_WIDTH = 512
MLP_CHUNK = 128
N_BRANCH = 4
BRANCH_WIDTH = 512
N_EXPERTS = 8
TOP_K = 2

SCAN_CHUNK = 64
LANES = 128
VMEM_LIMIT = 56 * 1024 * 1024

C_QKV, C_OG, C_XBC, C_Z, C_Q, C_U, C_SV, C_GATE, C_K, C_V = (
    0, 1536, 2048, 3072, 3584, 4096, 4608, 5120, 9216, 9344)
PROJ_MAIN = 9472
S_A, S_B, S_DT = 0, 8, 16
SMALL_ROWS = 32


def _cparams(sem):
    return pltpu.CompilerParams(dimension_semantics=sem, vmem_limit_bytes=VMEM_LIMIT)


def _bf(x):
    return x.astype(BF16)


def _dot(a, b):
    return jnp.dot(a, b, preferred_element_type=F32)


def _dot_nt(a, b):
    return lax.dot_general(a, b, (((1,), (1,)), ((), ())), preferred_element_type=F32)


def _dot_tn(a, b):
    return lax.dot_general(a, b, (((0,), (0,)), ((), ())), preferred_element_type=F32)


def _split(a):
    hi = a.astype(BF16)
    lo = (a - hi.astype(F32)).astype(BF16)
    return hi, lo


def _dot3(a, b, dot=_dot):
    ah, al = _split(a)
    bh, bl = _split(b)
    return dot(ah, bh) + (dot(al, bh) + dot(ah, bl))


def _silu(x):
    return x * (1.0 / (1.0 + jnp.exp(-x)))


def _sigmoid(x):
    return 1.0 / (1.0 + jnp.exp(-x))


def _softplus(x):
    return jnp.maximum(x, 0.0) + jnp.log1p(jnp.exp(-jnp.abs(x)))


def _gelu_tanh(x):
    return 0.5 * x * (1.0 + jnp.tanh(math.sqrt(2.0 / math.pi) * (x + 0.044715 * (x * x * x))))


def _ada_kernel(c_ref, w_ref, b_ref, o_ref):
    a = _silu(c_ref[...])
    o_ref[0] = _dot3(a, w_ref[0]) + b_ref[0]


def _ada_mods(c_rows, w_ada, b_ada):
    depth, _, n = w_ada.shape
    tn = 1536
    return pl.pallas_call(
        _ada_kernel,
        grid=(depth, n // tn),
        in_specs=[pl.BlockSpec((16, D_MODEL), lambda l, j: (0, 0)),
                  pl.BlockSpec((1, D_MODEL, tn), lambda l, j: (l, 0, j)),
                  pl.BlockSpec((1, 1, tn), lambda l, j: (l, 0, j))],
        out_specs=pl.BlockSpec((1, 16, tn), lambda l, j: (l, 0, j)),
        out_shape=jax.ShapeDtypeStruct((depth, 16, n), F32),
        compiler_params=_cparams(("parallel", "parallel")),
        name="ada_mods",
    )(c_rows, w_ada, b_ada.reshape(depth, 1, n))


def _normmod_kernel(x_ref, g_ref, mod_ref, h_ref, *, row0):
    x = x_ref[...]
    xn = x * lax.rsqrt(jnp.mean(x * x, axis=-1, keepdims=True) + EPS) * g_ref[...]
    mod = mod_ref[0]
    h = xn * (1.0 + mod[row0 + 1:row0 + 2]) + mod[row0:row0 + 1]
    h_ref[...] = h.astype(h_ref.dtype)


def _normmod_router_kernel(x_ref, g_ref, mod_ref, r_ref, h_ref, gate_ref, *, row0):
    x = x_ref[...]
    xn = x * lax.rsqrt(jnp.mean(x * x, axis=-1, keepdims=True) + EPS) * g_ref[...]
    mod = mod_ref[0]
    h = xn * (1.0 + mod[row0 + 1:row0 + 2]) + mod[row0:row0 + 1]
    h_ref[...] = h.astype(h_ref.dtype)
    logits = _dot3(h, r_ref[...])
    lane = lax.broadcasted_iota(jnp.int32, logits.shape, 1).astype(F32)
    logits = jnp.where(lane < N_EXPERTS, logits, -jnp.inf)
    m1 = jnp.max(logits, axis=-1, keepdims=True)
    i1 = jnp.min(jnp.where(logits == m1, lane, float(LANES)), axis=-1, keepdims=True)
    rest = jnp.where(lane == i1, -jnp.inf, logits)
    m2 = jnp.max(rest, axis=-1, keepdims=True)
    i2 = jnp.min(jnp.where(rest == m2, lane, float(LANES)), axis=-1, keepdims=True)
    e = jnp.exp(m2 - m1)
    w1 = 1.0 / (1.0 + e)
    w2 = e / (1.0 + e)
    gate_ref[...] = jnp.where(lane == i1, w1, 0.0) + jnp.where(lane == i2, w2, 0.0)


def _mod_index(n_lat_rows, seq, tm):
    def idx(i):
        r = i * tm
        return jnp.where(r < n_lat_rows, 1 + r // seq, 0)
    return idx


def _normmod(x, gain, mods, row0, n_lat_rows, seq, out_dtype, router=None):
    m = x.shape[0]
    tm = 512
    midx = _mod_index(n_lat_rows, seq, tm)
    in_specs = [pl.BlockSpec((tm, D_MODEL), lambda i: (i, 0)),
                pl.BlockSpec((1, D_MODEL), lambda i: (0, 0)),
                pl.BlockSpec((1, 6, D_MODEL), lambda i: (midx(i), 0, 0))]
    if router is None:
        return pl.pallas_call(
            functools.partial(_normmod_kernel, row0=row0),
            grid=(m // tm,),
            in_specs=in_specs,
            out_specs=pl.BlockSpec((tm, D_MODEL), lambda i: (i, 0)),
            out_shape=jax.ShapeDtypeStruct((m, D_MODEL), out_dtype),
            compiler_params=_cparams(("parallel",)),
            name="normmod",
        )(x, gain.reshape(1, D_MODEL), mods)
    return pl.pallas_call(
        functools.partial(_normmod_router_kernel, row0=row0),
        grid=(m // tm,),
        in_specs=in_specs + [pl.BlockSpec((D_MODEL, LANES), lambda i: (0, 0))],
        out_specs=[pl.BlockSpec((tm, D_MODEL), lambda i: (i, 0)),
                   pl.BlockSpec((tm, LANES), lambda i: (i, 0))],
        out_shape=[jax.ShapeDtypeStruct((m, D_MODEL), out_dtype),
                   jax.ShapeDtypeStruct((m, LANES), F32)],
        compiler_params=_cparams(("parallel",)),
        name="normmod_router",
    )(x, gain.reshape(1, D_MODEL), mods, router)


def _mm_kernel(a_ref, b_ref, o_ref):
    o_ref[...] = _dot(a_ref[...], b_ref[...]).astype(o_ref.dtype)


def _matmul(a, b, out_dtype, tm, tn):
    m, k = a.shape
    n = b.shape[1]
    return pl.pallas_call(
        _mm_kernel,
        grid=(n // tn, m // tm),
        in_specs=[pl.BlockSpec((tm, k), lambda j, i: (i, 0)),
                  pl.BlockSpec((k, tn), lambda j, i: (0, j))],
        out_specs=pl.BlockSpec((tm, tn), lambda j, i: (i, j)),
        out_shape=jax.ShapeDtypeStruct((m, n), out_dtype),
        compiler_params=_cparams(("parallel", "parallel")),
        name="proj_main",
    )(a, b)


def _proj_small_kernel(a_ref, w_ref, wt_ref, o_ref, ot_ref):
    a = a_ref[...]
    o_ref[...] = _dot(a, w_ref[...])
    ot_ref[...] = _dot_nt(wt_ref[...], a)


def _proj_small(h, w_small, w_small_t):
    m = h.shape[0]
    tm = 512
    return pl.pallas_call(
        _proj_small_kernel,
        grid=(m // tm,),
        in_specs=[pl.BlockSpec((tm, D_MODEL), lambda i: (i, 0)),
                  pl.BlockSpec((D_MODEL, LANES), lambda i: (0, 0)),
                  pl.BlockSpec((SMALL_ROWS, D_MODEL), lambda i: (0, 0))],
        out_specs=[pl.BlockSpec((tm, LANES), lambda i: (i, 0)),
                   pl.BlockSpec((SMALL_ROWS, tm), lambda i: (0, i))],
        out_shape=[jax.ShapeDtypeStruct((m, LANES), F32),
                   jax.ShapeDtypeStruct((SMALL_ROWS, m), F32)],
        compiler_params=_cparams(("parallel",)),
        name="proj_small",
    )(h, w_small, w_small_t)


CONV_TILE = 256
HALO = 16


def _conv_kernel(x_ref, prev_ref, next_ref, w_ref, b_ref, o_ref, *, tiles_per_seq, n_lat_tiles, l2_heads):
    j = pl.program_id(0)
    is_lat = j < n_lat_tiles
    first = jnp.logical_or(jnp.logical_not(is_lat), j % tiles_per_seq == 0)
    last = jnp.logical_or(jnp.logical_not(is_lat), j % tiles_per_seq == tiles_per_seq - 1)
    x = x_ref[...].astype(F32)
    rows = x.shape[0]
    prow = jnp.where(first, 0.0, prev_ref[...].astype(F32)[HALO - 1:HALO])
    nrow = jnp.where(last, 0.0, next_ref[...].astype(F32)[0:1])
    ridx = lax.broadcasted_iota(jnp.int32, x.shape, 0)
    xp = jnp.where(ridx == 0, prow, pltpu.roll(x, 1, axis=0))
    xn = jnp.where(ridx == rows - 1, nrow, pltpu.roll(x, rows - 1, axis=0))
    w = w_ref[...]
    y = _silu(w[0:1] * xp + w[1:2] * x + w[2:3] * xn + b_ref[...])
    if l2_heads:
        parts = []
        for hh in range(2 * GDN_HEADS):
            seg = y[:, hh * GDN_HEAD_DIM:(hh + 1) * GDN_HEAD_DIM]
            inv = lax.rsqrt(jnp.sum(seg * seg, axis=-1, keepdims=True) + EPS)
            if hh < GDN_HEADS:
                inv = inv * (GDN_HEAD_DIM ** -0.5)
            parts.append(seg * inv)
        parts.append(y[:, 2 * GDN_WIDTH:])
        y = jnp.concatenate(parts, axis=1)
    o_ref[...] = y.astype(o_ref.dtype)


def _conv_act(proj, col0, width, w, b, n_lat_rows, seq, l2_heads):
    m = proj.shape[0]
    n_tiles = m // CONV_TILE
    per_tile = CONV_TILE // HALO
    n_halo = m // HALO
    cb = col0 // width
    assert col0 % width == 0
    return pl.pallas_call(
        functools.partial(_conv_kernel, tiles_per_seq=seq // CONV_TILE,
                          n_lat_tiles=n_lat_rows // CONV_TILE, l2_heads=l2_heads),
        grid=(n_tiles,),
        in_specs=[pl.BlockSpec((CONV_TILE, width), lambda j: (j, cb)),
                  pl.BlockSpec((HALO, width), lambda j: (jnp.maximum(j * per_tile - 1, 0), cb)),
                  pl.BlockSpec((HALO, width), lambda j: (jnp.minimum((j + 1) * per_tile, n_halo - 1), cb)),
                  pl.BlockSpec((3, width), lambda j: (0, 0)),
                  pl.BlockSpec((1, width), lambda j: (0, 0))],
        out_specs=pl.BlockSpec((CONV_TILE, width), lambda j: (j, 0)),
        out_shape=jax.ShapeDtypeStruct((m, width), BF16),
        compiler_params=_cparams(("parallel",)),
        name="conv_act",
    )(proj, proj, proj, w, b.reshape(1, width))


def _chunk_maps(bsz, seq, ctx_len, cs):
    nc_ctx, nc_lat = ctx_len // cs, seq // cs
    lat_blocks = bsz * nc_lat

    def fwd(b, t):
        return jnp.where(t < nc_ctx, lat_blocks + b * nc_ctx + t, b * nc_lat + (t - nc_ctx))

    def bwd(b, t):
        return jnp.where(t < nc_ctx, lat_blocks + b * nc_ctx + (nc_ctx - 1 - t),
                         b * nc_lat + (nc_lat - 1 - (t - nc_ctx)))

    return fwd, bwd, nc_ctx + nc_lat


def _tri_masks(c):
    ri = lax.broadcasted_iota(jnp.int32, (c, c), 0)
    ci = lax.broadcasted_iota(jnp.int32, (c, c), 1)
    return (ri >= ci, ri <= ci), (ri > ci, ri < ci), ri == ci


def _unit_tri_inverse(nmat, eye):
    c = nmat.shape[0]
    p = eye - nmat
    m = nmat
    k = 2
    while k < c:
        m = _dot3(m, m)
        p = p + _dot3(p, m)
        k *= 2
    return p


def _gdn_kernel(qkv_f_ref, qkv_b_ref, sm_f_ref, sm_b_ref, smt_f_ref, smt_b_ref,
                pcol_ref, prow_a_ref, prow_b_ref, of_ref, ob_ref, state_ref):
    t = pl.program_id(1)

    @pl.when(t == 0)
    def _():
        state_ref[...] = jnp.zeros_like(state_ref)

    c = SCAN_CHUNK
    incl, strict, diag = _tri_masks(c)
    eye = diag.astype(F32)
    pcol = pcol_ref[...]
    alog_row, dtb_row = pcol[0:1], pcol[1:2]
    dirs = ((qkv_f_ref, sm_f_ref, smt_f_ref, of_ref), (qkv_b_ref, sm_b_ref, smt_b_ref, ob_ref))
    for d, (qkv_ref, sm_ref, smt_ref, o_ref) in enumerate(dirs):
        tri = incl[d].astype(F32)
        sm = sm_ref[...]
        smt = smt_ref[0]
        g_cols = -jnp.exp(alog_row) * _softplus(sm + dtb_row)
        beta_cols = _sigmoid(sm)
        g_rows = -jnp.exp(prow_a_ref[...]) * _softplus(smt + prow_b_ref[...])
        gcum_cols = _dot3(tri, g_cols)
        gcum_rows = _dot3(g_rows, tri, dot=_dot_nt)
        last = c - 1 if d == 0 else 0
        for hh in range(GDN_HEADS):
            col = S_A + d * GDN_HEADS + hh
            q = qkv_ref[:, hh * GDN_HEAD_DIM:(hh + 1) * GDN_HEAD_DIM]
            k = qkv_ref[:, GDN_WIDTH + hh * GDN_HEAD_DIM:GDN_WIDTH + (hh + 1) * GDN_HEAD_DIM]
            v = qkv_ref[:, 2 * GDN_WIDTH + hh * GDN_HEAD_DIM:2 * GDN_WIDTH + (hh + 1) * GDN_HEAD_DIM]
            kk = _dot_nt(k, k)
            qk = _dot_nt(q, k)
            gc = gcum_cols[:, col:col + 1]
            gr = gcum_rows[col:col + 1, :]
            decay = jnp.where(incl[d], jnp.exp(jnp.minimum(gc - gr, 0.0)), 0.0)
            beta = beta_cols[:, S_B + d * GDN_HEADS + hh:S_B + d * GDN_HEADS + hh + 1]
            nmat = jnp.where(strict[d], kk * decay, 0.0) * beta
            eg = jnp.exp(gc)
            kf = k.astype(F32)
            rhs = jnp.concatenate([v.astype(F32) * beta, kf * (beta * eg)], axis=1)
            sol = _dot3(_unit_tri_inverse(nmat, eye), rhs)
            u, w = sol[:, :GDN_HEAD_DIM], sol[:, GDN_HEAD_DIM:]
            g_last = gr[:, last:last + 1]
            q_in = q.astype(F32) * eg
            k_out = kf * jnp.exp(g_last - gc)
            s = state_ref[d, hh]
            sb = _bf(s)
            v_new = u - _dot(_bf(w), sb)
            vb = _bf(v_new)
            o = _dot(_bf(q_in), sb) + _dot(_bf(qk * decay), vb)
            state_ref[d, hh] = s * jnp.exp(g_last) + _dot_tn(_bf(k_out), vb)
            o_ref[:, hh * GDN_HEAD_DIM:(hh + 1) * GDN_HEAD_DIM] = o


def _gdn_scan(qkv, small, small_t, pcol, prow_a, prow_b, bsz, seq, ctx_len):
    m = qkv.shape[0]
    c = SCAN_CHUNK
    fwd, bwd, steps = _chunk_maps(bsz, seq, ctx_len, c)
    spec = lambda width, f: pl.BlockSpec((c, width), lambda b, t: (f(b, t), 0))
    spec_t = lambda f: pl.BlockSpec((1, SMALL_ROWS, c), lambda b, t: (f(b, t), 0, 0))
    const = lambda shape: pl.BlockSpec(shape, lambda b, t: (0,) * len(shape))
    return pl.pallas_call(
        _gdn_kernel,
        grid=(bsz, steps),
        in_specs=[spec(3 * GDN_WIDTH, fwd), spec(3 * GDN_WIDTH, bwd), spec(LANES, fwd), spec(LANES, bwd),
                  spec_t(fwd), spec_t(bwd), const((8, LANES)), const((SMALL_ROWS, c)), const((SMALL_ROWS, c))],
        out_specs=[spec(GDN_WIDTH, fwd), spec(GDN_WIDTH, bwd)],
        out_shape=[jax.ShapeDtypeStruct((m, GDN_WIDTH), F32)] * 2,
        scratch_shapes=[pltpu.VMEM((N_DIR, GDN_HEADS, GDN_HEAD_DIM, GDN_HEAD_DIM), F32)],
        compiler_params=_cparams(("parallel", "arbitrary")),
        name="gdn_scan",
    )(qkv, qkv, small, small, small_t, small_t, pcol, prow_a, prow_b)


def _ssd_kernel(xbc_f_ref, xbc_b_ref, sm_f_ref, sm_b_ref, smt_f_ref, smt_b_ref,
                pcol_ref, prow_a_ref, prow_b_ref, yf_ref, yb_ref, state_ref):
    t = pl.program_id(1)

    @pl.when(t == 0)
    def _():
        state_ref[...] = jnp.zeros_like(state_ref)

    c = SCAN_CHUNK
    incl, _, _ = _tri_masks(c)
    pcol = pcol_ref[...]
    alog_row, dtb_row = pcol[0:1], pcol[1:2]
    heads_per_group = SSD_HEADS // SSD_GROUPS
    dirs = ((xbc_f_ref, sm_f_ref, smt_f_ref, yf_ref), (xbc_b_ref, sm_b_ref, smt_b_ref, yb_ref))
    for d, (xbc_ref, sm_ref, smt_ref, y_ref) in enumerate(dirs):
        tri = incl[d].astype(F32)
        dt_cols = _softplus(sm_ref[...] + dtb_row)
        a_cols = -jnp.exp(alog_row) * dt_cols
        a_rows = -jnp.exp(prow_a_ref[...]) * _softplus(smt_ref[0] + prow_b_ref[...])
        acs_cols = _dot3(tri, a_cols)
        acs_rows = _dot3(a_rows, tri, dot=_dot_nt)
        last = c - 1 if d == 0 else 0
        for g in range(SSD_GROUPS):
            bm = xbc_ref[:, SSD_WIDTH + g * SSD_STATE:SSD_WIDTH + (g + 1) * SSD_STATE]
            cm = xbc_ref[:, SSD_WIDTH + (SSD_GROUPS + g) * SSD_STATE:SSD_WIDTH + (SSD_GROUPS + g + 1) * SSD_STATE]
            cb = _dot_nt(cm, bm)
            for r in range(heads_per_group):
                hh = g * heads_per_group + r
                col = S_DT + d * SSD_HEADS + hh
                ac = acs_cols[:, col:col + 1]
                ar = acs_rows[col:col + 1, :]
                lmat = jnp.where(incl[d], jnp.exp(jnp.minimum(ac - ar, 0.0)), 0.0)
                x = xbc_ref[:, hh * SSD_HEAD_DIM:(hh + 1) * SSD_HEAD_DIM].astype(F32)
                xdt = x * dt_cols[:, col:col + 1]
                a_last = ar[:, last:last + 1]
                y_diag = _dot(_bf(cb * lmat), _bf(xdt))
                cs = _dot_tn(_bf(xdt * jnp.exp(a_last - ac)), bm)
                s = state_ref[d, hh]
                y_off = _dot_nt(cm, _bf(s)) * jnp.exp(ac)
                state_ref[d, hh] = s * jnp.exp(a_last) + cs
                y_ref[:, hh * SSD_HEAD_DIM:(hh + 1) * SSD_HEAD_DIM] = y_diag + y_off


def _ssd_scan(xbc, small, small_t, pcol, prow_a, prow_b, bsz, seq, ctx_len):
    m = xbc.shape[0]
    c = SCAN_CHUNK
    fwd, bwd, steps = _chunk_maps(bsz, seq, ctx_len, c)
    spec = lambda width, f: pl.BlockSpec((c, width), lambda b, t: (f(b, t), 0))
    spec_t = lambda f: pl.BlockSpec((1, SMALL_ROWS, c), lambda b, t: (f(b, t), 0, 0))
    const = lambda shape: pl.BlockSpec(shape, lambda b, t: (0,) * len(shape))
    return pl.pallas_call(
        _ssd_kernel,
        grid=(bsz, steps),
        in_specs=[spec(SSD_CONV_CH, fwd), spec(SSD_CONV_CH, bwd), spec(LANES, fwd), spec(LANES, bwd),
                  spec_t(fwd), spec_t(bwd), const((8, LANES)), const((SMALL_ROWS, c)), const((SMALL_ROWS, c))],
        out_specs=[spec(SSD_WIDTH, fwd), spec(SSD_WIDTH, bwd)],
        out_shape=[jax.ShapeDtypeStruct((m, SSD_WIDTH), F32)] * 2,
        scratch_shapes=[pltpu.VMEM((N_DIR, SSD_HEADS, SSD_HEAD_DIM, SSD_STATE), F32)],
        compiler_params=_cparams(("parallel", "arbitrary")),
        name="ssd_scan",
    )(xbc, xbc, small, small, small_t, small_t, pcol, prow_a, prow_b)


def _rope(t, cos, sin_signed):
    width = t.shape[1]
    lane = lax.broadcasted_iota(jnp.int32, t.shape, 1)
    partner = jnp.where(lane % 32 < ROPE_FREQS, pltpu.roll(t, width - ROPE_FREQS, axis=1),
                        pltpu.roll(t, ROPE_FREQS, axis=1))
    return t * cos + partner * sin_signed


def _attn_kernel(sink_ref, q_ref, k0_ref, k1_ref, k2_ref, v0_ref, v1_ref, v2_ref, kc_ref, vc_ref,
                 cq_ref, sq_ref, c0_ref, c1_ref, c2_ref, s0_ref, s1_ref, s2_ref, o_ref, *, n_lat_blocks):
    i = pl.program_id(1)
    blk = ATT_BLOCK
    is_lat = i < n_lat_blocks
    rep = ATT_HEADS // ATT_KV_HEADS
    cq = jnp.concatenate([cq_ref[...]] * (ATT_WIDTH // LANES), axis=1)
    sq = jnp.concatenate([sq_ref[...]] * (ATT_WIDTH // LANES), axis=1)
    q = _rope(q_ref[...].astype(F32), cq, sq) * (ATT_HEAD_DIM ** -0.5)
    kwin = jnp.concatenate([_rope(k0_ref[...].astype(F32), c0_ref[...], s0_ref[...]),
                            _rope(k1_ref[...].astype(F32), c1_ref[...], s1_ref[...]),
                            _rope(k2_ref[...].astype(F32), c2_ref[...], s2_ref[...])], axis=0)
    vwin = jnp.concatenate([v0_ref[...], v1_ref[...], v2_ref[...]], axis=0)
    kc = kc_ref[...]
    vc = vc_ref[...]
    rows = rep * blk
    qpos = i * blk + lax.broadcasted_iota(jnp.int32, (rows, 3 * blk), 0) % blk
    kpos = (i - 1) * blk + lax.broadcasted_iota(jnp.int32, (rows, 3 * blk), 1)
    valid = (jnp.abs(qpos - kpos) <= WINDOW) & (kpos >= 0) & (kpos < n_lat_blocks * blk) & is_lat
    rgrp = lax.broadcasted_iota(jnp.int32, (rows, 1), 0) // blk
    for g in range(ATT_KV_HEADS):
        qg = q[:, g * rep * ATT_HEAD_DIM:(g + 1) * rep * ATT_HEAD_DIM]
        qs = _bf(jnp.concatenate([qg[:, r * ATT_HEAD_DIM:(r + 1) * ATT_HEAD_DIM] for r in range(rep)], axis=0))
        kg = _bf(kwin[:, g * ATT_HEAD_DIM:(g + 1) * ATT_HEAD_DIM])
        vg = vwin[:, g * ATT_HEAD_DIM:(g + 1) * ATT_HEAD_DIM]
        s_loc = jnp.where(valid, _dot_nt(qs, kg), -jnp.inf)
        s_ctx = _dot_nt(qs, kc[:, g * ATT_HEAD_DIM:(g + 1) * ATT_HEAD_DIM])
        sink = jnp.zeros((rows, 1), F32)
        for r in range(rep):
            sink = jnp.where(rgrp == r, sink_ref[g * rep + r], sink)
        mx = jnp.maximum(jnp.maximum(jnp.max(s_loc, axis=-1, keepdims=True),
                                     jnp.max(s_ctx, axis=-1, keepdims=True)), sink)
        p_loc = jnp.exp(s_loc - mx)
        p_ctx = jnp.exp(s_ctx - mx)
        den = (jnp.sum(p_loc, axis=-1, keepdims=True) + jnp.sum(p_ctx, axis=-1, keepdims=True)
               + jnp.exp(sink - mx))
        o = (_dot(_bf(p_loc), vg) + _dot(_bf(p_ctx), vc[:, g * ATT_HEAD_DIM:(g + 1) * ATT_HEAD_DIM])) / den
        for r in range(rep):
            hcol = (g * rep + r) * ATT_HEAD_DIM
            o_ref[:, hcol:hcol + ATT_HEAD_DIM] = o[r * blk:(r + 1) * blk].astype(o_ref.dtype)


def _attention(proj, sink, cos_t, sin_t, bsz, seq, ctx_len):
    m = proj.shape[0]
    blk = ATT_BLOCK
    nl, ncx = seq // blk, ctx_len // blk
    lat_blocks = bsz * nl

    def qrow(b, i):
        return jnp.where(i < nl, b * nl + i, lat_blocks + b * ncx + (i - nl))

    def krow(off):
        return lambda b, i, s: (b * nl + jnp.clip(i + off, 0, nl - 1), C_K // ATT_KV_WIDTH)

    def vrow(off):
        return lambda b, i, s: (b * nl + jnp.clip(i + off, 0, nl - 1), C_V // ATT_KV_WIDTH)

    def trow(off):
        return lambda b, i, s: (jnp.clip(i + off, 0, nl - 1), 0)

    kv_spec = lambda f: pl.BlockSpec((blk, ATT_KV_WIDTH), f)
    tab = lambda f: pl.BlockSpec((blk, LANES), f)
    ctx_blk = lambda colblk: pl.BlockSpec((ctx_len, ATT_KV_WIDTH),
                                          lambda b, i, s: (bsz * seq // ctx_len + b, colblk))
    grid_spec = pltpu.PrefetchScalarGridSpec(
        num_scalar_prefetch=1,
        grid=(bsz, nl + ncx),
        in_specs=[pl.BlockSpec((blk, ATT_WIDTH), lambda b, i, s: (qrow(b, i), C_Q // ATT_WIDTH)),
                  kv_spec(krow(-1)), kv_spec(krow(0)), kv_spec(krow(1)),
                  kv_spec(vrow(-1)), kv_spec(vrow(0)), kv_spec(vrow(1)),
                  ctx_blk(C_K // ATT_KV_WIDTH), ctx_blk(C_V // ATT_KV_WIDTH),
                  tab(lambda b, i, s: (i, 0)), tab(lambda b, i, s: (i, 0)),
                  tab(trow(-1)), tab(trow(0)), tab(trow(1)),
                  tab(trow(-1)), tab(trow(0)), tab(trow(1))],
        out_specs=pl.BlockSpec((blk, ATT_WIDTH), lambda b, i, s: (qrow(b, i), 0)),
    )
    return pl.pallas_call(
        functools.partial(_attn_kernel, n_lat_blocks=nl),
        grid_spec=grid_spec,
        out_shape=jax.ShapeDtypeStruct((m, ATT_WIDTH), BF16),
        compiler_params=_cparams(("parallel", "parallel")),
        name="window_attn",
    )(sink, proj, proj, proj, proj, proj, proj, proj, proj, proj,
      cos_t, sin_t, cos_t, cos_t, cos_t, sin_t, sin_t, sin_t)


def _gmlp_kernel(u_ref, sv_ref, ws_ref, bs_ref, o_ref):
    u = _gelu_tanh(u_ref[...].astype(F32))
    v = _gelu_tanh(sv_ref[...].astype(F32))
    for g in range(MLP_GROUPS):
        vg = v[:, g * MLP_GROUP_DIM:(g + 1) * MLP_GROUP_DIM]
        mu = jnp.mean(vg, axis=-1, keepdims=True)
        vc = vg - mu
        vn = vc * lax.rsqrt(jnp.mean(vc * vc, axis=-1, keepdims=True) + EPS)
        sp = _dot(ws_ref[g], _bf(vn)) + bs_ref[g]
        o_ref[:, g * MLP_GROUP_DIM:(g + 1) * MLP_GROUP_DIM] = (
            u[:, g * MLP_GROUP_DIM:(g + 1) * MLP_GROUP_DIM] * sp).astype(o_ref.dtype)


def _gmlp(proj, ws, bs_b):
    m = proj.shape[0]
    ck = MLP_CHUNK
    return pl.pallas_call(
        _gmlp_kernel,
        grid=(m // ck,),
        in_specs=[pl.BlockSpec((ck, MLP_WIDTH), lambda i: (i, C_U // MLP_WIDTH)),
                  pl.BlockSpec((ck, MLP_WIDTH), lambda i: (i, C_SV // MLP_WIDTH)),
                  pl.BlockSpec((MLP_GROUPS, ck, ck), lambda i: (0, 0, 0)),
                  pl.BlockSpec((MLP_GROUPS, ck, MLP_GROUP_DIM), lambda i: (0, 0, 0))],
        out_specs=pl.BlockSpec((ck, MLP_WIDTH), lambda i: (i, 0)),
        out_shape=jax.ShapeDtypeStruct((m, MLP_WIDTH), BF16),
        compiler_params=_cparams(("parallel",)),
        name="gmlp",
    )(proj, proj, ws, bs_b)


def _group_rms(y, gain, group):
    parts = []
    for s in range(0, y.shape[1], group):
        seg = y[:, s:s + group]
        parts.append(seg * lax.rsqrt(jnp.mean(seg * seg, axis=-1, keepdims=True) + EPS))
    return jnp.concatenate(parts, axis=1) * gain


def _merge_kernel(gf_ref, gb_ref, og_ref, sf_ref, sb_ref, xc_ref, z_ref, att_ref, mlp_ref,
                  g0_ref, g1_ref, g2_ref, g3_ref, x_ref, mod_ref, wb_ref, wo_ref,
                  gn_ref, dsk_ref, sn_ref, o_ref):
    ya = _group_rms(gf_ref[...] + gb_ref[...], gn_ref[...], GDN_HEAD_DIM) * _silu(og_ref[...].astype(F32))
    yb = sf_ref[...] + sb_ref[...] + dsk_ref[...] * xc_ref[...].astype(F32)
    yb = _group_rms(yb * _silu(z_ref[...].astype(F32)), sn_ref[...], SSD_WIDTH // SSD_GROUPS)
    ys = (_bf(ya), _bf(yb), att_ref[...], mlp_ref[...])
    acc = None
    for mi, (y, g_ref) in enumerate(zip(ys, (g0_ref, g1_ref, g2_ref, g3_ref))):
        term = _sigmoid(g_ref[...].astype(F32)) * _dot(y, wb_ref[mi])
        acc = term if acc is None else acc + term
    out = _dot(_bf(acc), wo_ref[...])
    o_ref[...] = x_ref[...] + mod_ref[0][2:3] * out


def _merge(gdn_f, gdn_b, ssd_f, ssd_b, xbc, att, mlp, proj, x, mods, wb, wo, gn, dsk, sn, n_lat_rows, seq):
    m = x.shape[0]
    tm = 256
    midx = _mod_index(n_lat_rows, seq, tm)
    half = lambda cb=0: pl.BlockSpec((tm, BRANCH_WIDTH), lambda i: (i, cb))
    gate = lambda k: pl.BlockSpec((tm, D_MODEL), lambda i: (i, C_GATE // D_MODEL + k))
    row = lambda w: pl.BlockSpec((1, w), lambda i: (0, 0))
    return pl.pallas_call(
        _merge_kernel,
        grid=(m // tm,),
        in_specs=[half(), half(), half(C_OG // BRANCH_WIDTH), half(), half(), half(0), half(C_Z // BRANCH_WIDTH),
                  half(), half(), gate(0), gate(1), gate(2), gate(3),
                  pl.BlockSpec((tm, D_MODEL), lambda i: (i, 0)),
                  pl.BlockSpec((1, 6, D_MODEL), lambda i: (midx(i), 0, 0)),
                  pl.BlockSpec((N_BRANCH, BRANCH_WIDTH, D_MODEL), lambda i: (0, 0, 0)),
                  pl.BlockSpec((D_MODEL, D_MODEL), lambda i: (0, 0)),
                  row(BRANCH_WIDTH), row(BRANCH_WIDTH), row(BRANCH_WIDTH)],
        out_specs=pl.BlockSpec((tm, D_MODEL), lambda i: (i, 0)),
        out_shape=jax.ShapeDtypeStruct((m, D_MODEL), F32),
        compiler_params=_cparams(("parallel",)),
        name="merge",
    )(gdn_f, gdn_b, proj, ssd_f, ssd_b, xbc, proj, att, mlp, proj, proj, proj, proj, x, mods, wb, wo, gn, dsk, sn)


def _ffn_kernel(h_ref, gate_ref, wg_ref, wu_ref, wd_ref, x_ref, mod_ref, o_ref, acc_ref):
    e = pl.program_id(1)
    f = pl.program_id(2)

    @pl.when(jnp.logical_and(e == 0, f == 0))
    def _():
        acc_ref[...] = jnp.zeros_like(acc_ref)

    h = h_ref[...]
    a = _silu(_dot(h, wg_ref[0])) * _dot(h, wu_ref[0])
    y = _dot(_bf(a), wd_ref[0])
    lane = lax.broadcasted_iota(jnp.int32, gate_ref.shape, 1)
    ge = jnp.sum(jnp.where(lane == e, gate_ref[...], 0.0), axis=-1, keepdims=True)
    acc_ref[...] += ge * y

    @pl.when(jnp.logical_and(e == pl.num_programs(1) - 1, f == pl.num_programs(2) - 1))
    def _():
        o_ref[...] = x_ref[...] + mod_ref[0][5:6] * acc_ref[...]


def _ffn(h, gates, wg, wu, wd, x, mods, tf, n_lat_rows, seq):
    m = h.shape[0]
    n_e, _, ff = wg.shape
    tm = 512
    midx = _mod_index(n_lat_rows, seq, tm)
    return pl.pallas_call(
        _ffn_kernel,
        grid=(m // tm, n_e, ff // tf),
        in_specs=[pl.BlockSpec((tm, D_MODEL), lambda i, e, f: (i, 0)),
                  pl.BlockSpec((tm, LANES), lambda i, e, f: (i, 0)),
                  pl.BlockSpec((1, D_MODEL, tf), lambda i, e, f: (e, 0, f)),
                  pl.BlockSpec((1, D_MODEL, tf), lambda i, e, f: (e, 0, f)),
                  pl.BlockSpec((1, tf, D_MODEL), lambda i, e, f: (e, f, 0)),
                  pl.BlockSpec((tm, D_MODEL), lambda i, e, f: (i, 0)),
                  pl.BlockSpec((1, 6, D_MODEL), lambda i, e, f: (midx(i), 0, 0))],
        out_specs=pl.BlockSpec((tm, D_MODEL), lambda i, e, f: (i, 0)),
        out_shape=jax.ShapeDtypeStruct((m, D_MODEL), F32),
        scratch_shapes=[pltpu.VMEM((tm, D_MODEL), F32)],
        compiler_params=_cparams(("parallel", "arbitrary", "arbitrary")),
        name="ffn",
    )(h, gates, wg, wu, wd, x, mods)


def _final_norm_kernel(x_ref, g_ref, o_ref):
    x = x_ref[...]
    o_ref[...] = x * lax.rsqrt(jnp.mean(x * x, axis=-1, keepdims=True) + EPS) * g_ref[...]


def _final_norm(x, gain, rows):
    tm = 512
    return pl.pallas_call(
        _final_norm_kernel,
        grid=(rows // tm,),
        in_specs=[pl.BlockSpec((tm, D_MODEL), lambda i: (i, 0)),
                  pl.BlockSpec((1, D_MODEL), lambda i: (0, 0))],
        out_specs=pl.BlockSpec((tm, D_MODEL), lambda i: (i, 0)),
        out_shape=jax.ShapeDtypeStruct((rows, D_MODEL), F32),
        compiler_params=_cparams(("parallel",)),
        name="final_norm",
    )(x, gain.reshape(1, D_MODEL))


def _split_w_in(w):
    sizes = (1536, 512, 8, 8, 1024, 512, 16, 512, 128, 128, 512, 512, 4096)
    offs = np.concatenate([[0], np.cumsum(sizes)])
    (qkv, og, a, b, xbc, z, dt, q, k, v, u, sv, gate) = [w[:, int(offs[i]):int(offs[i + 1])] for i in range(13)]
    main = jnp.concatenate([qkv, og, xbc, z, q, u, sv, gate, k, v], axis=1).astype(BF16)
    small = jnp.concatenate([a, b, dt, jnp.zeros((w.shape[0], LANES - SMALL_ROWS), w.dtype)], axis=1).astype(BF16)
    return main, small, small[:, :SMALL_ROWS].T


def _scan_params(gdn_a_log, gdn_dt_bias, ssd_a_log, ssd_dt_bias):
    zeros8 = jnp.zeros((8,), F32)
    alog = jnp.concatenate([gdn_a_log.reshape(-1), zeros8, ssd_a_log.reshape(-1)]).astype(F32)
    dtb = jnp.concatenate([gdn_dt_bias.reshape(-1), zeros8, ssd_dt_bias.reshape(-1)]).astype(F32)
    pad = jnp.zeros((LANES - SMALL_ROWS,), F32)
    pcol = jnp.zeros((8, LANES), F32).at[0].set(jnp.concatenate([alog, pad])).at[1].set(jnp.concatenate([dtb, pad]))
    prow_a = jnp.broadcast_to(alog[:, None], (SMALL_ROWS, SCAN_CHUNK))
    prow_b = jnp.broadcast_to(dtb[:, None], (SMALL_ROWS, SCAN_CHUNK))
    return pcol, prow_a, prow_b


def _rope_tables(seq, ctx_len):
    rows = seq // GRID_W
    row = jnp.repeat(jnp.arange(rows), GRID_W)
    col = jnp.tile(jnp.arange(GRID_W), rows)
    inv_freq = ROPE_BASE ** (-jnp.arange(ROPE_FREQS, dtype=F32) / ROPE_FREQS)
    ang = jnp.stack([row, col], axis=-1).astype(F32)[..., None] * inv_freq
    cos, sin = jnp.cos(ang), jnp.sin(ang)
    cos_h = jnp.concatenate([cos, cos], axis=-1).reshape(seq, ATT_HEAD_DIM)
    sin_h = jnp.concatenate([-sin, sin], axis=-1).reshape(seq, ATT_HEAD_DIM)
    cos_t = jnp.concatenate([jnp.tile(cos_h, (1, LANES // ATT_HEAD_DIM)), jnp.ones((ctx_len, LANES), F32)], axis=0)
    sin_t = jnp.concatenate([jnp.tile(sin_h, (1, LANES // ATT_HEAD_DIM)), jnp.zeros((ctx_len, LANES), F32)], axis=0)
    return cos_t, sin_t


def kernel(x, c, ctx, c_ctx, w_ada, b_ada, norm1, norm2, w_in, gdn_conv, gdn_A_log, gdn_dt_bias, gdn_norm,
           ssd_conv, ssd_conv_b, ssd_A_log, ssd_dt_bias, ssd_D, ssd_norm, attn_sink, mlp_ws, mlp_bs,
           w_branch, w_out, ffn_wg, ffn_wu, ffn_wd, moe_router, moe_wg, moe_wu, moe_wd, final_norm):
    bsz, seq, _ = x.shape
    ctx_len = ctx.shape[1]
    depth = w_ada.shape[0]
    n_lat = bsz * seq
    assert seq % 512 == 0 and ctx_len == CONV_TILE and (bsz * ctx_len) % 512 == 0

    xf = jnp.concatenate([x.reshape(n_lat, D_MODEL), ctx.reshape(bsz * ctx_len, D_MODEL)], axis=0)
    c_rows = jnp.zeros((16, D_MODEL), F32).at[0].set(c_ctx).at[1:1 + bsz].set(c)
    mods_all = _ada_mods(c_rows, w_ada, b_ada).reshape(depth, 16, 6, D_MODEL)
    cos_t, sin_t = _rope_tables(seq, ctx_len)
    ones_gate = jnp.ones((xf.shape[0], LANES), F32)

    for i in range(depth):
        mods = mods_all[i]
        w_main, w_small, w_small_t = _split_w_in(w_in[i])
        pcol, prow_a, prow_b = _scan_params(gdn_A_log[i], gdn_dt_bias[i], ssd_A_log[i], ssd_dt_bias[i])

        h = _normmod(xf, norm1[i], mods, 0, n_lat, seq, BF16)
        proj = _matmul(h, w_main, BF16, 512, PROJ_MAIN // 2)
        small, small_t = _proj_small(h, w_small, w_small_t)
        small_t = small_t.reshape(SMALL_ROWS, -1, SCAN_CHUNK).transpose(1, 0, 2)

        qkv = _conv_act(proj, C_QKV, 3 * GDN_WIDTH, gdn_conv[i], jnp.zeros((3 * GDN_WIDTH,), F32), n_lat, seq, True)
        xbc = _conv_act(proj, C_XBC, SSD_CONV_CH, ssd_conv[i], ssd_conv_b[i], n_lat, seq, False)
        gdn_f, gdn_b = _gdn_scan(qkv, small, small_t, pcol, prow_a, prow_b, bsz, seq, ctx_len)
        ssd_f, ssd_b = _ssd_scan(xbc, small, small_t, pcol, prow_a, prow_b, bsz, seq, ctx_len)
        att = _attention(proj, attn_sink[i].astype(F32), cos_t, sin_t, bsz, seq, ctx_len)
        bs_b = jnp.broadcast_to(mlp_bs[i][:, :, None], (MLP_GROUPS, MLP_CHUNK, MLP_GROUP_DIM)).astype(F32)
        mlp = _gmlp(proj, mlp_ws[i].astype(BF16), bs_b)

        xf = _merge(gdn_f, gdn_b, ssd_f, ssd_b, xbc, att, mlp, proj, xf, mods,
                    w_branch[i].astype(BF16), w_out[i].astype(BF16),
                    jnp.tile(gdn_norm[i], GDN_HEADS).reshape(1, GDN_WIDTH).astype(F32),
                    jnp.repeat(ssd_D[i], SSD_HEAD_DIM).reshape(1, SSD_WIDTH).astype(F32),
                    ssd_norm[i].reshape(1, SSD_WIDTH).astype(F32), n_lat, seq)

        j = i // 2
        if i % 2 == 0:
            h2 = _normmod(xf, norm2[i], mods, 3, n_lat, seq, BF16)
            xf = _ffn(h2, ones_gate, ffn_wg[j][None].astype(BF16), ffn_wu[j][None].astype(BF16),
                      ffn_wd[j][None].astype(BF16), xf, mods, 1408, n_lat, seq)
        else:
            router = jnp.concatenate([moe_router[j], jnp.zeros((D_MODEL, LANES - N_EXPERTS), F32)], axis=1)
            h2, gates = _normmod(xf, norm2[i], mods, 3, n_lat, seq, BF16, router=router)
            xf = _ffn(h2, gates, moe_wg[j].astype(BF16), moe_wu[j].astype(BF16), moe_wd[j].astype(BF16),
                      xf, mods, 896, n_lat, seq)

    return _final_norm(xf, final_norm, n_lat).reshape(bsz, seq, D_MODEL)
```

```python
import functools
import math

import jax
import jax.numpy as jnp
import numpy as np
from jax import lax
from jax.experimental import pallas as pl
from jax.experimental.pallas import tpu as pltpu

F32 = jnp.float32
BF16 = jnp.bfloat16

D_MODEL = 1024
GRID_W = 64
EPS = 1e-6
N_DIR = 2

GDN_HEADS = 4
GDN_HEAD_DIM = 128
GDN_WIDTH = 512
SSD_HEADS = 8
SSD_HEAD_DIM = 64
SSD_WIDTH = 512
SSD_GROUPS = 2
SSD_STATE = 128
SSD_CONV_CH = 1024
ATT_HEADS = 8
ATT_KV_HEADS = 2
ATT_HEAD_DIM = 64
ATT_WIDTH = 512
ATT_KV_WIDTH = 128
WINDOW = 128
ATT_BLOCK = 128
ROPE_BASE = 10000.0
ROPE_FREQS = 16
MLP_GROUPS = 4
MLP_GROUP_DIM = 128
MLP_WIDTH = 512
MLP_CHUNK = 128
N_BRANCH = 4
BRANCH_WIDTH = 512
N_EXPERTS = 8
TOP_K = 2

SCAN_CHUNK = 64
LANES = 128
VMEM_LIMIT = 56 * 1024 * 1024

C_QKV, C_OG, C_XBC, C_Z, C_Q, C_U, C_SV, C_GATE, C_K, C_V = (
    0, 1536, 2048, 3072, 3584, 4096, 4608, 5120, 9216, 9344)
PROJ_MAIN = 9472
S_A, S_B, S_DT = 0, 8, 16
SMALL_ROWS = 32


def _cparams(sem):
    return pltpu.CompilerParams(dimension_semantics=sem, vmem_limit_bytes=VMEM_LIMIT)


def _bf(x):
    return x.astype(BF16)


def _dot(a, b):
    return jnp.dot(a, b, preferred_element_type=F32)


def _dot_nt(a, b):
    return lax.dot_general(a, b, (((1,), (1,)), ((), ())), preferred_element_type=F32)


def _dot_tn(a, b):
    return lax.dot_general(a, b, (((0,), (0,)), ((), ())), preferred_element_type=F32)


def _split(a):
    hi = a.astype(BF16)
    lo = (a - hi.astype(F32)).astype(BF16)
    return hi, lo


def _dot3(a, b, dot=_dot):
    ah, al = _split(a)
    bh, bl = _split(b)
    return dot(ah, bh) + (dot(al, bh) + dot(ah, bl))


def _silu(x):
    return x * (1.0 / (1.0 + jnp.exp(-x)))


def _sigmoid(x):
    return 1.0 / (1.0 + jnp.exp(-x))


def _softplus(x):
    return jnp.maximum(x, 0.0) + jnp.log1p(jnp.exp(-jnp.abs(x)))


def _gelu_tanh(x):
    return 0.5 * x * (1.0 + jnp.tanh(math.sqrt(2.0 / math.pi) * (x + 0.044715 * (x * x * x))))


def _ada_kernel(c_ref, w_ref, b_ref, o_ref):
    a = _silu(c_ref[...])
    o_ref[0] = _dot3(a, w_ref[0]) + b_ref[0]


def _ada_mods(c_rows, w_ada, b_ada):
    depth, _, n = w_ada.shape
    tn = 1536
    return pl.pallas_call(
        _ada_kernel,
        grid=(depth, n // tn),
        in_specs=[pl.BlockSpec((16, D_MODEL), lambda l, j: (0, 0)),
                  pl.BlockSpec((1, D_MODEL, tn), lambda l, j: (l, 0, j)),
                  pl.BlockSpec((1, 1, tn), lambda l, j: (l, 0, j))],
        out_specs=pl.BlockSpec((1, 16, tn), lambda l, j: (l, 0, j)),
        out_shape=jax.ShapeDtypeStruct((depth, 16, n), F32),
        compiler_params=_cparams(("parallel", "parallel")),
        name="ada_mods",
    )(c_rows, w_ada, b_ada.reshape(depth, 1, n))


def _normmod_kernel(x_ref, g_ref, mod_ref, h_ref, *, row0):
    x = x_ref[...]
    xn = x * lax.rsqrt(jnp.mean(x * x, axis=-1, keepdims=True) + EPS) * g_ref[...]
    mod = mod_ref[0]
    h = xn * (1.0 + mod[row0 + 1:row0 + 2]) + mod[row0:row0 + 1]
    h_ref[...] = h.astype(h_ref.dtype)


def _normmod_router_kernel(x_ref, g_ref, mod_ref, r_ref, h_ref, gate_ref, cnt_ref, *, row0):
    x = x_ref[...]
    xn = x * lax.rsqrt(jnp.mean(x * x, axis=-1, keepdims=True) + EPS) * g_ref[...]
    mod = mod_ref[0]
    h = xn * (1.0 + mod[row0 + 1:row0 + 2]) + mod[row0:row0 + 1]
    h_ref[...] = h.astype(h_ref.dtype)
    logits = _dot3(h, r_ref[...])
    lane = lax.broadcasted_iota(jnp.int32, logits.shape, 1).astype(F32)
    logits = jnp.where(lane < N_EXPERTS, logits, -jnp.inf)
    m1 = jnp.max(logits, axis=-1, keepdims=True)
    i1 = jnp.min(jnp.where(logits == m1, lane, float(LANES)), axis=-1, keepdims=True)
    rest = jnp.where(lane == i1, -jnp.inf, logits)
    m2 = jnp.max(rest, axis=-1, keepdims=True)
    i2 = jnp.min(jnp.where(rest == m2, lane, float(LANES)), axis=-1, keepdims=True)
    e = jnp.exp(m2 - m1)
    w1 = 1.0 / (1.0 + e)
    w2 = e / (1.0 + e)
    gate = jnp.where(lane == i1, w1, 0.0) + jnp.where(lane == i2, w2, 0.0)
    gate_ref[...] = gate
    cnt = jnp.sum((gate != 0.0).astype(F32), axis=0, keepdims=True)
    cnt_ref[0] = jnp.broadcast_to(cnt, (8, LANES))


def _mod_index(n_lat_rows, seq, tm):
    def idx(i):
        r = i * tm
        return jnp.where(r < n_lat_rows, 1 + r // seq, 0)
    return idx


ROUTER_TILE = 512


def _normmod(x, gain, mods, row0, n_lat_rows, seq, out_dtype, router=None):
    m = x.shape[0]
    tm = ROUTER_TILE
    midx = _mod_index(n_lat_rows, seq, tm)
    in_specs = [pl.BlockSpec((tm, D_MODEL), lambda i: (i, 0)),
                pl.BlockSpec((1, D_MODEL), lambda i: (0, 0)),
                pl.BlockSpec((1, 6, D_MODEL), lambda i: (midx(i), 0, 0))]
    if router is None:
        return pl.pallas_call(
            functools.partial(_normmod_kernel, row0=row0),
            grid=(m // tm,),
            in_specs=in_specs,
            out_specs=pl.BlockSpec((tm, D_MODEL), lambda i: (i, 0)),
            out_shape=jax.ShapeDtypeStruct((m, D_MODEL), out_dtype),
            compiler_params=_cparams(("parallel",)),
            name="normmod",
        )(x, gain.reshape(1, D_MODEL), mods)
    return pl.pallas_call(
        functools.partial(_normmod_router_kernel, row0=row0),
        grid=(m // tm,),
        in_specs=in_specs + [pl.BlockSpec((D_MODEL, LANES), lambda i: (0, 0))],
        out_specs=[pl.BlockSpec((tm, D_MODEL), lambda i: (i, 0)),
                   pl.BlockSpec((tm, LANES), lambda i: (i, 0)),
                   pl.BlockSpec((1, 8, LANES), lambda i: (i, 0, 0))],
        out_shape=[jax.ShapeDtypeStruct((m, D_MODEL), out_dtype),
                   jax.ShapeDtypeStruct((m, LANES), F32),
                   jax.ShapeDtypeStruct((m // tm, 8, LANES), F32)],
        compiler_params=_cparams(("parallel",)),
        name="normmod_router",
    )(x, gain.reshape(1, D_MODEL), mods, router)


def _mm_kernel(a_ref, b_ref, o_ref):
    o_ref[...] = _dot(a_ref[...], b_ref[...]).astype(o_ref.dtype)


def _matmul(a, b, out_dtype, tm, tn):
    m, k = a.shape
    n = b.shape[1]
    return pl.pallas_call(
        _mm_kernel,
        grid=(n // tn, m // tm),
        in_specs=[pl.BlockSpec((tm, k), lambda j, i: (i, 0)),
                  pl.BlockSpec((k, tn), lambda j, i: (0, j))],
        out_specs=pl.BlockSpec((tm, tn), lambda j, i: (i, j)),
        out_shape=jax.ShapeDtypeStruct((m, n), out_dtype),
        compiler_params=_cparams(("parallel", "parallel")),
        name="proj_main",
    )(a, b)


def _proj_small_kernel(a_ref, w_ref, wt_ref, o_ref, ot_ref):
    a = a_ref[...]
    o_ref[...] = _dot(a, w_ref[...])
    ot_ref[...] = _dot_nt(wt_ref[...], a)


def _proj_small(h, w_small, w_small_t):
    m = h.shape[0]
    tm = 512
    return pl.pallas_call(
        _proj_small_kernel,
        grid=(m // tm,),
        in_specs=[pl.BlockSpec((tm, D_MODEL), lambda i: (i, 0)),
                  pl.BlockSpec((D_MODEL, LANES), lambda i: (0, 0)),
                  pl.BlockSpec((SMALL_ROWS, D_MODEL), lambda i: (0, 0))],
        out_specs=[pl.BlockSpec((tm, LANES), lambda i: (i, 0)),
                   pl.BlockSpec((SMALL_ROWS, tm), lambda i: (0, i))],
        out_shape=[jax.ShapeDtypeStruct((m, LANES), F32),
                   jax.ShapeDtypeStruct((SMALL_ROWS, m), F32)],
        compiler_params=_cparams(("parallel",)),
        name="proj_small",
    )(h, w_small, w_small_t)


CONV_TILE = 256
HALO = 16


def _conv_kernel(x_ref, prev_ref, next_ref, w_ref, b_ref, o_ref, *, tiles_per_seq, n_lat_tiles, l2_heads):
    j = pl.program_id(0)
    is_lat = j < n_lat_tiles
    first = jnp.logical_or(jnp.logical_not(is_lat), j % tiles_per_seq == 0)
    last = jnp.logical_or(jnp.logical_not(is_lat), j % tiles_per_seq == tiles_per_seq - 1)
    x = x_ref[...].astype(F32)
    rows = x.shape[0]
    prow = jnp.where(first, 0.0, prev_ref[...].astype(F32)[HALO - 1:HALO])
    nrow = jnp.where(last, 0.0, next_ref[...].astype(F32)[0:1])
    ridx = lax.broadcasted_iota(jnp.int32, x.shape, 0)
    xp = jnp.where(ridx == 0, prow, pltpu.roll(x, 1, axis=0))
    xn = jnp.where(ridx == rows - 1, nrow, pltpu.roll(x, rows - 1, axis=0))
    w = w_ref[...]
    y = _silu(w[0:1] * xp + w[1:2] * x + w[2:3] * xn + b_ref[...])
    if l2_heads:
        parts = []
        for hh in range(2 * GDN_HEADS):
            seg = y[:, hh * GDN_HEAD_DIM:(hh + 1) * GDN_HEAD_DIM]
            inv = lax.rsqrt(jnp.sum(seg * seg, axis=-1, keepdims=True) + EPS)
            if hh < GDN_HEADS:
                inv = inv * (GDN_HEAD_DIM ** -0.5)
            parts.append(seg * inv)
        parts.append(y[:, 2 * GDN_WIDTH:])
        y = jnp.concatenate(parts, axis=1)
    o_ref[...] = y.astype(o_ref.dtype)


def _conv_act(proj, col0, width, w, b, n_lat_rows, seq, l2_heads):
    m = proj.shape[0]
    n_tiles = m // CONV_TILE
    per_tile = CONV_TILE // HALO
    n_halo = m // HALO
    cb = col0 // width
    assert col0 % width == 0
    return pl.pallas_call(
        functools.partial(_conv_kernel, tiles_per_seq=seq // CONV_TILE,
                          n_lat_tiles=n_lat_rows // CONV_TILE, l2_heads=l2_heads),
        grid=(n_tiles,),
        in_specs=[pl.BlockSpec((CONV_TILE, width), lambda j: (j, cb)),
                  pl.BlockSpec((HALO, width), lambda j: (jnp.maximum(j * per_tile - 1, 0), cb)),
                  pl.BlockSpec((HALO, width), lambda j: (jnp.minimum((j + 1) * per_tile, n_halo - 1), cb)),
                  pl.BlockSpec((3, width), lambda j: (0, 0)),
                  pl.BlockSpec((1, width), lambda j: (0, 0))],
        out_specs=pl.BlockSpec((CONV_TILE, width), lambda j: (j, 0)),
        out_shape=jax.ShapeDtypeStruct((m, width), BF16),
        compiler_params=_cparams(("parallel",)),
        name="conv_act",
    )(proj, proj, proj, w, b.reshape(1, width))


def _chunk_maps(bsz, seq, ctx_len, cs):
    nc_ctx, nc_lat = ctx_len // cs, seq // cs
    lat_blocks = bsz * nc_lat

    def fwd(b, t):
        return jnp.where(t < nc_ctx, lat_blocks + b * nc_ctx + t, b * nc_lat + (t - nc_ctx))

    def bwd(b, t):
        return jnp.where(t < nc_ctx, lat_blocks + b * nc_ctx + (nc_ctx - 1 - t),
                         b * nc_lat + (nc_lat - 1 - (t - nc_ctx)))

    return fwd, bwd, nc_ctx + nc_lat


def _tri_masks(c):
    ri = lax.broadcasted_iota(jnp.int32, (c, c), 0)
    ci = lax.broadcasted_iota(jnp.int32, (c, c), 1)
    return (ri >= ci, ri <= ci), (ri > ci, ri < ci), ri == ci


def _bdot(a, b):
    return lax.dot_general(a, b, (((2,), (1,)), ((0,), (0,))), preferred_element_type=F32)


def _bdot1(a, b):
    return _bdot(_bf(a), _bf(b))


def _unit_tri_solve(nmat, rhs):
    c = nmat.shape[1]
    ri = lax.broadcasted_iota(jnp.int32, (c, c), 0)
    ci = lax.broadcasted_iota(jnp.int32, (c, c), 1)
    same = lambda size: ((ri // size) == (ci // size))[None]
    base = 16
    m = jnp.where(same(base), nmat, 0.0)
    t = (ri == ci).astype(F32)[None] - m
    k = 2
    while k < base:
        m = _bdot1(m, m)
        t = t + _bdot1(t, m)
        k *= 2
    size = base
    while size < c:
        off = jnp.where(jnp.logical_and(same(2 * size), jnp.logical_not(same(size))), nmat, 0.0)
        t = t - _bdot1(_bdot1(t, off), t)
        size *= 2
    x0 = _bdot1(t, rhs)
    nh, nl = _split(nmat)
    xh, xl = _split(x0)
    resid = rhs - x0 - (_bdot(nh, xh) + (_bdot(nl, xh) + _bdot(nh, xl)))
    return x0 + _bdot1(t, resid)


GDN_LOCAL_CHUNKS = 2


def _gdn_local_kernel(qkv_ref, sm_ref, smt_ref, pcol_ref, prow_a_ref, prow_b_ref,
                      a1_ref, u_ref, a2_ref, eg_ref):
    c = SCAN_CHUNK
    incl, strict, _ = _tri_masks(c)
    tri = incl[0].astype(F32)
    eye_k = (lax.broadcasted_iota(jnp.int32, (GDN_HEAD_DIM, GDN_HEAD_DIM), 0)
             == lax.broadcasted_iota(jnp.int32, (GDN_HEAD_DIM, GDN_HEAD_DIM), 1)).astype(BF16)
    pcol = pcol_ref[...]
    alog_row, dtb_row = pcol[0:1], pcol[1:2]
    nmats, rhss, rest = [], [], []
    for ck in range(GDN_LOCAL_CHUNKS):
        rows = slice(ck * c, (ck + 1) * c)
        sm = sm_ref[rows, :]
        smt = smt_ref[ck]
        g_cols = -jnp.exp(alog_row) * _softplus(sm + dtb_row)
        beta_cols = _sigmoid(sm)
        g_rows = -jnp.exp(prow_a_ref[...]) * _softplus(smt + prow_b_ref[...])
        pre_cols = _dot3(tri, g_cols)
        pre_rows = _dot3(g_rows, tri, dot=_dot_nt)
        tot_cols, tot_rows = pre_cols[c - 1:c, :], pre_rows[:, c - 1:c]
        gcum_cols = (pre_cols, tot_cols - pre_cols + g_cols)
        gcum_rows = (pre_rows, tot_rows - pre_rows + g_rows)
        for hh in range(GDN_HEADS):
            q = qkv_ref[rows, hh * GDN_HEAD_DIM:(hh + 1) * GDN_HEAD_DIM]
            k = qkv_ref[rows, GDN_WIDTH + hh * GDN_HEAD_DIM:GDN_WIDTH + (hh + 1) * GDN_HEAD_DIM]
            v = qkv_ref[rows, 2 * GDN_WIDTH + hh * GDN_HEAD_DIM:2 * GDN_WIDTH + (hh + 1) * GDN_HEAD_DIM]
            kk = _dot_nt(k, k)
            qk = _dot_nt(q, k)
            kt = _dot_nt(eye_k, k)
            qf, kf, vf = q.astype(F32), k.astype(F32), v.astype(F32)
            for d in range(N_DIR):
                col = S_A + d * GDN_HEADS + hh
                gc = gcum_cols[d][:, col:col + 1]
                gr = gcum_rows[d][col:col + 1, :]
                g_last = tot_rows[col:col + 1, :]
                decay = jnp.where(incl[d], jnp.exp(jnp.minimum(gc - gr, 0.0)), 0.0)
                beta = beta_cols[:, S_B + d * GDN_HEADS + hh:S_B + d * GDN_HEADS + hh + 1]
                eg = jnp.exp(gc)
                nmats.append(jnp.where(strict[d], kk * decay, 0.0) * beta)
                rhss.append(jnp.concatenate([vf * beta, kf * (beta * eg)], axis=1))
                rest.append((d, ck, hh, qf * eg, qk * decay, kt * jnp.exp(g_last - gr), jnp.exp(g_last)))
    sol = _unit_tri_solve(jnp.stack(nmats), jnp.stack(rhss))
    for idx, (d, ck, hh, q_in, qkm, kout_t, eg_last) in enumerate(rest):
        u_ref[d, ck, hh] = _bf(sol[idx, :, :GDN_HEAD_DIM])
        a1_ref[d, ck, hh, 0:c, :] = _bf(sol[idx, :, GDN_HEAD_DIM:])
        a1_ref[d, ck, hh, c:2 * c, :] = _bf(q_in)
        a2_ref[d, ck, hh, 0:c, :] = _bf(qkm)
        a2_ref[d, ck, hh, c:c + GDN_HEAD_DIM, :] = _bf(kout_t)
        row = d * GDN_HEADS + hh
        eg_ref[ck, row:row + 1, :] = jnp.broadcast_to(eg_last, (1, LANES))


def _gdn_scan_kernel(a1f_ref, a1b_ref, uf_ref, ub_ref, a2f_ref, a2b_ref, egf_ref, egb_ref,
                     of_ref, ob_ref, state_ref):
    t = pl.program_id(1)

    @pl.when(t == 0)
    def _():
        state_ref[...] = jnp.zeros_like(state_ref)

    c = SCAN_CHUNK
    dirs = ((a1f_ref, uf_ref, a2f_ref, egf_ref, of_ref), (a1b_ref, ub_ref, a2b_ref, egb_ref, ob_ref))
    chains = [(d, hh) for d in range(N_DIR) for hh in range(GDN_HEADS)]
    states = {ch: state_ref[ch[0], ch[1]] for ch in chains}
    m1 = {(d, hh): _dot(dirs[d][0][0, 0, hh], _bf(states[d, hh])) for d, hh in chains}
    m2 = {(d, hh): _dot(dirs[d][2][0, 0, hh], _bf(dirs[d][1][0, 0, hh].astype(F32) - m1[d, hh][0:c]))
          for d, hh in chains}
    for d, hh in chains:
        row = d * GDN_HEADS + hh
        dirs[d][4][:, hh * GDN_HEAD_DIM:(hh + 1) * GDN_HEAD_DIM] = m1[d, hh][c:2 * c] + m2[d, hh][0:c]
        state_ref[d, hh] = states[d, hh] * dirs[d][3][0, row:row + 1, :] + m2[d, hh][c:c + GDN_HEAD_DIM]


def _gdn_mixer(qkv, small, small_t, pcol, prow_a, prow_b, bsz, seq, ctx_len):
    m = qkv.shape[0]
    c = SCAN_CHUNK
    n_chunks = m // c
    cb = GDN_LOCAL_CHUNKS
    hd = GDN_HEAD_DIM
    const = lambda shape: pl.BlockSpec(shape, lambda i: (0,) * len(shape))
    a1, u, a2, eg = pl.pallas_call(
        _gdn_local_kernel,
        grid=(n_chunks // cb,),
        in_specs=[pl.BlockSpec((cb * c, 3 * GDN_WIDTH), lambda i: (i, 0)),
                  pl.BlockSpec((cb * c, LANES), lambda i: (i, 0)),
                  pl.BlockSpec((cb, SMALL_ROWS, c), lambda i: (i, 0, 0)),
                  const((8, LANES)), const((SMALL_ROWS, c)), const((SMALL_ROWS, c))],
        out_specs=[pl.BlockSpec((N_DIR, cb, GDN_HEADS, 2 * c, hd), lambda i: (0, i, 0, 0, 0)),
                   pl.BlockSpec((N_DIR, cb, GDN_HEADS, c, hd), lambda i: (0, i, 0, 0, 0)),
                   pl.BlockSpec((N_DIR, cb, GDN_HEADS, c + hd, c), lambda i: (0, i, 0, 0, 0)),
                   pl.BlockSpec((cb, 8, LANES), lambda i: (i, 0, 0))],
        out_shape=[jax.ShapeDtypeStruct((N_DIR, n_chunks, GDN_HEADS, 2 * c, hd), BF16),
                   jax.ShapeDtypeStruct((N_DIR, n_chunks, GDN_HEADS, c, hd), BF16),
                   jax.ShapeDtypeStruct((N_DIR, n_chunks, GDN_HEADS, c + hd, c), BF16),
                   jax.ShapeDtypeStruct((n_chunks, 8, LANES), F32)],
        compiler_params=_cparams(("parallel",)),
        name="gdn_local",
    )(qkv, small, small_t, pcol, prow_a, prow_b)

    fwd, bwd, steps = _chunk_maps(bsz, seq, ctx_len, c)
    per_dir = lambda rows, cols: [pl.BlockSpec((1, 1, GDN_HEADS, rows, cols), lambda b, t: (0, fwd(b, t), 0, 0, 0)),
                                  pl.BlockSpec((1, 1, GDN_HEADS, rows, cols), lambda b, t: (1, bwd(b, t), 0, 0, 0))]
    out = lambda f: pl.BlockSpec((c, GDN_WIDTH), lambda b, t: (f(b, t), 0))
    return pl.pallas_call(
        _gdn_scan_kernel,
        grid=(bsz, steps),
        in_specs=(per_dir(2 * c, hd) + per_dir(c, hd) + per_dir(c + hd, c)
                  + [pl.BlockSpec((1, 8, LANES), lambda b, t: (fwd(b, t), 0, 0)),
                     pl.BlockSpec((1, 8, LANES), lambda b, t: (bwd(b, t), 0, 0))]),
        out_specs=[out(fwd), out(bwd)],
        out_shape=[jax.ShapeDtypeStruct((m, GDN_WIDTH), F32)] * 2,
        scratch_shapes=[pltpu.VMEM((N_DIR, GDN_HEADS, hd, hd), F32)],
        compiler_params=_cparams(("parallel", "arbitrary")),
        name="gdn_scan",
    )(a1, a1, u, u, a2, a2, eg, eg)


def _ssd_kernel(xbc_f_ref, xbc_b_ref, sm_f_ref, sm_b_ref, smt_f_ref, smt_b_ref,
                pcol_ref, prow_a_ref, prow_b_ref, yf_ref, yb_ref, state_ref):
    t = pl.program_id(1)

    @pl.when(t == 0)
    def _():
        state_ref[...] = jnp.zeros_like(state_ref)

    c = SCAN_CHUNK
    incl, _, _ = _tri_masks(c)
    pcol = pcol_ref[...]
    alog_row, dtb_row = pcol[0:1], pcol[1:2]
    heads_per_group = SSD_HEADS // SSD_GROUPS
    dirs = ((xbc_f_ref, sm_f_ref, smt_f_ref, yf_ref), (xbc_b_ref, sm_b_ref, smt_b_ref, yb_ref))
    for d, (xbc_ref, sm_ref, smt_ref, y_ref) in enumerate(dirs):
        tri = incl[d].astype(F32)
        dt_cols = _softplus(sm_ref[...] + dtb_row)
        a_cols = -jnp.exp(alog_row) * dt_cols
        a_rows = -jnp.exp(prow_a_ref[...]) * _softplus(smt_ref[0] + prow_b_ref[...])
        acs_cols = _dot3(tri, a_cols)
        acs_rows = _dot3(a_rows, tri, dot=_dot_nt)
        last = c - 1 if d == 0 else 0
        for g in range(SSD_GROUPS):
            bm = xbc_ref[:, SSD_WIDTH + g * SSD_STATE:SSD_WIDTH + (g + 1) * SSD_STATE]
            cm = xbc_ref[:, SSD_WIDTH + (SSD_GROUPS + g) * SSD_STATE:SSD_WIDTH + (SSD_GROUPS + g + 1) * SSD_STATE]
            cb = _dot_nt(cm, bm)
            for r in range(heads_per_group):
                hh = g * heads_per_group + r
                col = S_DT + d * SSD_HEADS + hh
                ac = acs_cols[:, col:col + 1]
                ar = acs_rows[col:col + 1, :]
                lmat = jnp.where(incl[d], jnp.exp(jnp.minimum(ac - ar, 0.0)), 0.0)
                x = xbc_ref[:, hh * SSD_HEAD_DIM:(hh + 1) * SSD_HEAD_DIM].astype(F32)
                xdt = x * dt_cols[:, col:col + 1]
                a_last = ar[:, last:last + 1]
                y_diag = _dot(_bf(cb * lmat), _bf(xdt))
                cs = _dot_tn(_bf(xdt * jnp.exp(a_last - ac)), bm)
                s = state_ref[d, hh]
                y_off = _dot_nt(cm, _bf(s)) * jnp.exp(ac)
                state_ref[d, hh] = s * jnp.exp(a_last) + cs
                y_ref[:, hh * SSD_HEAD_DIM:(hh + 1) * SSD_HEAD_DIM] = y_diag + y_off


def _ssd_scan(xbc, small, small_t, pcol, prow_a, prow_b, bsz, seq, ctx_len):
    m = xbc.shape[0]
    c = SCAN_CHUNK
    fwd, bwd, steps = _chunk_maps(bsz, seq, ctx_len, c)
    spec = lambda width, f: pl.BlockSpec((c, width), lambda b, t: (f(b, t), 0))
    spec_t = lambda f: pl.BlockSpec((1, SMALL_ROWS, c), lambda b, t: (f(b, t), 0, 0))
    const = lambda shape: pl.BlockSpec(shape, lambda b, t: (0,) * len(shape))
    return pl.pallas_call(
        _ssd_kernel,
        grid=(bsz, steps),
        in_specs=[spec(SSD_CONV_CH, fwd), spec(SSD_CONV_CH, bwd), spec(LANES, fwd), spec(LANES, bwd),
                  spec_t(fwd), spec_t(bwd), const((8, LANES)), const((SMALL_ROWS, c)), const((SMALL_ROWS, c))],
        out_specs=[spec(SSD_WIDTH, fwd), spec(SSD_WIDTH, bwd)],
        out_shape=[jax.ShapeDtypeStruct((m, SSD_WIDTH), F32)] * 2,
        scratch_shapes=[pltpu.VMEM((N_DIR, SSD_HEADS, SSD_HEAD_DIM, SSD_STATE), F32)],
        compiler_params=_cparams(("parallel", "arbitrary")),
        name="ssd_scan",
    )(xbc, xbc, small, small, small_t, small_t, pcol, prow_a, prow_b)


def _rope(t, cos, sin_signed):
    width = t.shape[1]
    lane = lax.broadcasted_iota(jnp.int32, t.shape, 1)
    partner = jnp.where(lane % 32 < ROPE_FREQS, pltpu.roll(t, width - ROPE_FREQS, axis=1),
                        pltpu.roll(t, ROPE_FREQS, axis=1))
    return t * cos + partner * sin_signed


def _attn_kernel(sink_ref, q_ref, k0_ref, k1_ref, k2_ref, v0_ref, v1_ref, v2_ref, kc_ref, vc_ref,
                 cq_ref, sq_ref, c0_ref, c1_ref, c2_ref, s0_ref, s1_ref, s2_ref, o_ref, *, n_lat_blocks):
    i = pl.program_id(1)
    blk = ATT_BLOCK
    is_lat = i < n_lat_blocks
    rep = ATT_HEADS // ATT_KV_HEADS
    cq = jnp.concatenate([cq_ref[...]] * (ATT_WIDTH // LANES), axis=1)
    sq = jnp.concatenate([sq_ref[...]] * (ATT_WIDTH // LANES), axis=1)
    q = _rope(q_ref[...].astype(F32), cq, sq) * (ATT_HEAD_DIM ** -0.5)
    kwin = jnp.concatenate([_rope(k0_ref[...].astype(F32), c0_ref[...], s0_ref[...]),
                            _rope(k1_ref[...].astype(F32), c1_ref[...], s1_ref[...]),
                            _rope(k2_ref[...].astype(F32), c2_ref[...], s2_ref[...])], axis=0)
    vwin = jnp.concatenate([v0_ref[...], v1_ref[...], v2_ref[...]], axis=0)
    kc = kc_ref[...]
    vc = vc_ref[...]
    rows = rep * blk
    qpos = i * blk + lax.broadcasted_iota(jnp.int32, (rows, 3 * blk), 0) % blk
    kpos = (i - 1) * blk + lax.broadcasted_iota(jnp.int32, (rows, 3 * blk), 1)
    valid = (jnp.abs(qpos - kpos) <= WINDOW) & (kpos >= 0) & (kpos < n_lat_blocks * blk) & is_lat
    rgrp = lax.broadcasted_iota(jnp.int32, (rows, 1), 0) // blk
    for g in range(ATT_KV_HEADS):
        qg = q[:, g * rep * ATT_HEAD_DIM:(g + 1) * rep * ATT_HEAD_DIM]
        qs = _bf(jnp.concatenate([qg[:, r * ATT_HEAD_DIM:(r + 1) * ATT_HEAD_DIM] for r in range(rep)], axis=0))
        kg = _bf(kwin[:, g * ATT_HEAD_DIM:(g + 1) * ATT_HEAD_DIM])
        vg = vwin[:, g * ATT_HEAD_DIM:(g + 1) * ATT_HEAD_DIM]
        s_loc = jnp.where(valid, _dot_nt(qs, kg), -jnp.inf)
        s_ctx = _dot_nt(qs, kc[:, g * ATT_HEAD_DIM:(g + 1) * ATT_HEAD_DIM])
        sink = jnp.zeros((rows, 1), F32)
        for r in range(rep):
            sink = jnp.where(rgrp == r, sink_ref[g * rep + r], sink)
        mx = jnp.maximum(jnp.maximum(jnp.max(s_loc, axis=-1, keepdims=True),
                                     jnp.max(s_ctx, axis=-1, keepdims=True)), sink)
        p_loc = jnp.exp(s_loc - mx)
        p_ctx = jnp.exp(s_ctx - mx)
        den = (jnp.sum(p_loc, axis=-1, keepdims=True) + jnp.sum(p_ctx, axis=-1, keepdims=True)
               + jnp.exp(sink - mx))
        o = (_dot(_bf(p_loc), vg) + _dot(_bf(p_ctx), vc[:, g * ATT_HEAD_DIM:(g + 1) * ATT_HEAD_DIM])) / den
        for r in range(rep):
            hcol = (g * rep + r) * ATT_HEAD_DIM
            o_ref[:, hcol:hcol + ATT_HEAD_DIM] = o[r * blk:(r + 1) * blk].astype(o_ref.dtype)


def _attention(proj, sink, cos_t, sin_t, bsz, seq, ctx_len):
    m = proj.shape[0]
    blk = ATT_BLOCK
    nl, ncx = seq // blk, ctx_len // blk
    lat_blocks = bsz * nl

    def qrow(b, i):
        return jnp.where(i < nl, b * nl + i, lat_blocks + b * ncx + (i - nl))

    def krow(off):
        return lambda b, i, s: (b * nl + jnp.clip(i + off, 0, nl - 1), C_K // ATT_KV_WIDTH)

    def vrow(off):
        return lambda b, i, s: (b * nl + jnp.clip(i + off, 0, nl - 1), C_V // ATT_KV_WIDTH)

    def trow(off):
        return lambda b, i, s: (jnp.clip(i + off, 0, nl - 1), 0)

    kv_spec = lambda f: pl.BlockSpec((blk, ATT_KV_WIDTH), f)
    tab = lambda f: pl.BlockSpec((blk, LANES), f)
    ctx_blk = lambda colblk: pl.BlockSpec((ctx_len, ATT_KV_WIDTH),
                                          lambda b, i, s: (bsz * seq // ctx_len + b, colblk))
    grid_spec = pltpu.PrefetchScalarGridSpec(
        num_scalar_prefetch=1,
        grid=(bsz, nl + ncx),
        in_specs=[pl.BlockSpec((blk, ATT_WIDTH), lambda b, i, s: (qrow(b, i), C_Q // ATT_WIDTH)),
                  kv_spec(krow(-1)), kv_spec(krow(0)), kv_spec(krow(1)),
                  kv_spec(vrow(-1)), kv_spec(vrow(0)), kv_spec(vrow(1)),
                  ctx_blk(C_K // ATT_KV_WIDTH), ctx_blk(C_V // ATT_KV_WIDTH),
                  tab(lambda b, i, s: (i, 0)), tab(lambda b, i, s: (i, 0)),
                  tab(trow(-1)), tab(trow(0)), tab(trow(1)),
                  tab(trow(-1)), tab(trow(0)), tab(trow(1))],
        out_specs=pl.BlockSpec((blk, ATT_WIDTH), lambda b, i, s: (qrow(b, i), 0)),
    )
    return pl.pallas_call(
        functools.partial(_attn_kernel, n_lat_blocks=nl),
        grid_spec=grid_spec,
        out_shape=jax.ShapeDtypeStruct((m, ATT_WIDTH), BF16),
        compiler_params=_cparams(("parallel", "parallel")),
        name="window_attn",
    )(sink, proj, proj, proj, proj, proj, proj, proj, proj, proj,
      cos_t, sin_t, cos_t, cos_t, cos_t, sin_t, sin_t, sin_t)


def _gmlp_kernel(u_ref, sv_ref, ws_ref, bs_ref, o_ref):
    u = _gelu_tanh(u_ref[...].astype(F32))
    v = _gelu_tanh(sv_ref[...].astype(F32))
    for g in range(MLP_GROUPS):
        vg = v[:, g * MLP_GROUP_DIM:(g + 1) * MLP_GROUP_DIM]
        mu = jnp.mean(vg, axis=-1, keepdims=True)
        vc = vg - mu
        vn = vc * lax.rsqrt(jnp.mean(vc * vc, axis=-1, keepdims=True) + EPS)
        sp = _dot(ws_ref[g], _bf(vn)) + bs_ref[g]
        o_ref[:, g * MLP_GROUP_DIM:(g + 1) * MLP_GROUP_DIM] = (
            u[:, g * MLP_GROUP_DIM:(g + 1) * MLP_GROUP_DIM] * sp).astype(o_ref.dtype)


def _gmlp(proj, ws, bs_b):
    m = proj.shape[0]
    ck = MLP_CHUNK
    return pl.pallas_call(
        _gmlp_kernel,
        grid=(m // ck,),
        in_specs=[pl.BlockSpec((ck, MLP_WIDTH), lambda i: (i, C_U // MLP_WIDTH)),
                  pl.BlockSpec((ck, MLP_WIDTH), lambda i: (i, C_SV // MLP_WIDTH)),
                  pl.BlockSpec((MLP_GROUPS, ck, ck), lambda i: (0, 0, 0)),
                  pl.BlockSpec((MLP_GROUPS, ck, MLP_GROUP_DIM), lambda i: (0, 0, 0))],
        out_specs=pl.BlockSpec((ck, MLP_WIDTH), lambda i: (i, 0)),
        out_shape=jax.ShapeDtypeStruct((m, MLP_WIDTH), BF16),
        compiler_params=_cparams(("parallel",)),
        name="gmlp",
    )(proj, proj, ws, bs_b)


def _group_rms(y, gain, group):
    parts = []
    for s in range(0, y.shape[1], group):
        seg = y[:, s:s + group]
        parts.append(seg * lax.rsqrt(jnp.mean(seg * seg, axis=-1, keepdims=True) + EPS))
    return jnp.concatenate(parts, axis=1) * gain


def _merge_kernel(gf_ref, gb_ref, og_ref, sf_ref, sb_ref, xc_ref, z_ref, att_ref, mlp_ref,
                  g0_ref, g1_ref, g2_ref, g3_ref, x_ref, mod_ref, wb_ref, wo_ref,
                  gn_ref, dsk_ref, sn_ref, o_ref):
    ya = _group_rms(gf_ref[...] + gb_ref[...], gn_ref[...], GDN_HEAD_DIM) * _silu(og_ref[...].astype(F32))
    yb = sf_ref[...] + sb_ref[...] + dsk_ref[...] * xc_ref[...].astype(F32)
    yb = _group_rms(yb * _silu(z_ref[...].astype(F32)), sn_ref[...], SSD_WIDTH // SSD_GROUPS)
    ys = (_bf(ya), _bf(yb), att_ref[...], mlp_ref[...])
    acc = None
    for mi, (y, g_ref) in enumerate(zip(ys, (g0_ref, g1_ref, g2_ref, g3_ref))):
        term = _sigmoid(g_ref[...].astype(F32)) * _dot(y, wb_ref[mi])
        acc = term if acc is None else acc + term
    out = _dot(_bf(acc), wo_ref[...])
    o_ref[...] = x_ref[...] + mod_ref[0][2:3] * out


def _merge(gdn_f, gdn_b, ssd_f, ssd_b, xbc, att, mlp, proj, x, mods, wb, wo, gn, dsk, sn, n_lat_rows, seq):
    m = x.shape[0]
    tm = 256
    midx = _mod_index(n_lat_rows, seq, tm)
    half = lambda cb=0: pl.BlockSpec((tm, BRANCH_WIDTH), lambda i: (i, cb))
    gate = lambda k: pl.BlockSpec((tm, D_MODEL), lambda i: (i, C_GATE // D_MODEL + k))
    row = lambda w: pl.BlockSpec((1, w), lambda i: (0, 0))
    return pl.pallas_call(
        _merge_kernel,
        grid=(m // tm,),
        in_specs=[half(), half(), half(C_OG // BRANCH_WIDTH), half(), half(), half(0), half(C_Z // BRANCH_WIDTH),
                  half(), half(), gate(0), gate(1), gate(2), gate(3),
                  pl.BlockSpec((tm, D_MODEL), lambda i: (i, 0)),
                  pl.BlockSpec((1, 6, D_MODEL), lambda i: (midx(i), 0, 0)),
                  pl.BlockSpec((N_BRANCH, BRANCH_WIDTH, D_MODEL), lambda i: (0, 0, 0)),
                  pl.BlockSpec((D_MODEL, D_MODEL), lambda i: (0, 0)),
                  row(BRANCH_WIDTH), row(BRANCH_WIDTH), row(BRANCH_WIDTH)],
        out_specs=pl.BlockSpec((tm, D_MODEL), lambda i: (i, 0)),
        out_shape=jax.ShapeDtypeStruct((m, D_MODEL), F32),
        compiler_params=_cparams(("parallel",)),
        name="merge",
    )(gdn_f, gdn_b, proj, ssd_f, ssd_b, xbc, proj, att, mlp, proj, proj, proj, proj, x, mods, wb, wo, gn, dsk, sn)


def _ffn_kernel(h_ref, wg_ref, wu_ref, wd_ref, x_ref, mod_ref, o_ref, acc_ref):
    f = pl.program_id(1)

    @pl.when(f == 0)
    def _():
        acc_ref[...] = jnp.zeros_like(acc_ref)

    h = h_ref[...]
    a = _silu(_dot(h, wg_ref[...])) * _dot(h, wu_ref[...])
    acc_ref[...] += _dot(_bf(a), wd_ref[...])

    @pl.when(f == pl.num_programs(1) - 1)
    def _():
        o_ref[...] = x_ref[...] + mod_ref[0][5:6] * acc_ref[...]


def _ffn(h, wg, wu, wd, x, mods, tf, n_lat_rows, seq):
    m = h.shape[0]
    ff = wg.shape[1]
    tm = 512
    midx = _mod_index(n_lat_rows, seq, tm)
    return pl.pallas_call(
        _ffn_kernel,
        grid=(m // tm, ff // tf),
        in_specs=[pl.BlockSpec((tm, D_MODEL), lambda i, f: (i, 0)),
                  pl.BlockSpec((D_MODEL, tf), lambda i, f: (0, f)),
                  pl.BlockSpec((D_MODEL, tf), lambda i, f: (0, f)),
                  pl.BlockSpec((tf, D_MODEL), lambda i, f: (f, 0)),
                  pl.BlockSpec((tm, D_MODEL), lambda i, f: (i, 0)),
                  pl.BlockSpec((1, 6, D_MODEL), lambda i, f: (midx(i), 0, 0))],
        out_specs=pl.BlockSpec((tm, D_MODEL), lambda i, f: (i, 0)),
        out_shape=jax.ShapeDtypeStruct((m, D_MODEL), F32),
        scratch_shapes=[pltpu.VMEM((tm, D_MODEL), F32)],
        compiler_params=_cparams(("parallel", "arbitrary")),
        name="ffn",
    )(h, wg, wu, wd, x, mods)


MOE_EXTRA = 128


def _moe_kernel(cnt_ref, h_ref, gate_ref, wg_ref, wu_ref, wd_ref, x_ref, mod_ref, o_ref,
                hc_ref, y_ref, rcol_ref, rrow_ref, *, main_rows):
    i, e, f = pl.program_id(0), pl.program_id(1), pl.program_id(2)
    n_e, n_f = pl.num_programs(1), pl.num_programs(2)
    t = h_ref.shape[0]
    cnt = cnt_ref[i * N_EXPERTS + e]
    n_extra = jnp.maximum(cnt - main_rows + MOE_EXTRA - 1, 0) // MOE_EXTRA

    @pl.when(jnp.logical_and(e == 0, f == 0))
    def _():
        o_ref[...] = jnp.zeros_like(o_ref)
        ri = lax.broadcasted_iota(jnp.int32, (t, t), 0)
        ci = lax.broadcasted_iota(jnp.int32, (t, t), 1)
        before = (ci < ri).astype(BF16)
        sel = (gate_ref[...] != 0.0).astype(BF16)
        eye = (lax.broadcasted_iota(jnp.int32, (LANES, LANES), 0)
               == lax.broadcasted_iota(jnp.int32, (LANES, LANES), 1)).astype(BF16)
        sel_t = _dot_nt(eye, sel)
        rank_c = _dot(before, sel)
        rank_r = _dot_nt(_bf(sel_t), before)
        rcol_ref[...] = jnp.where(sel > 0, rank_c, -1.0)
        rrow_ref[...] = jnp.where(sel_t > 0, rank_r, -1.0)[0:N_EXPERTS]

    def compact(row0, rows):
        rr = rrow_ref[pl.ds(e, 1), :]
        want = (row0 + lax.broadcasted_iota(jnp.int32, (rows, t), 0)).astype(F32)
        hc_ref[pl.ds(row0, rows), :] = _bf(_dot((rr == want).astype(BF16), h_ref[...]))
        y_ref[pl.ds(row0, rows), :] = jnp.zeros((rows, D_MODEL), F32)

    def expert(row0, rows):
        hc = hc_ref[pl.ds(row0, rows), :]
        a = _silu(_dot(hc, wg_ref[0])) * _dot(hc, wu_ref[0])
        y_ref[pl.ds(row0, rows), :] += _dot(_bf(a), wd_ref[0])

    def scatter(row0, rows, rc, ge):
        want = (row0 + lax.broadcasted_iota(jnp.int32, (t, rows), 1)).astype(F32)
        o_ref[...] += ge * _dot((rc == want).astype(BF16), _bf(y_ref[pl.ds(row0, rows), :]))

    def extra_rows(j):
        return pl.multiple_of(main_rows + j * MOE_EXTRA, 16)

    @pl.when(f == 0)
    def _():
        compact(0, main_rows)
        lax.fori_loop(0, n_extra, lambda j, c: (compact(extra_rows(j), MOE_EXTRA), c)[1], 0)

    expert(0, main_rows)
    lax.fori_loop(0, n_extra, lambda j, c: (expert(extra_rows(j), MOE_EXTRA), c)[1], 0)

    @pl.when(f == n_f - 1)
    def _():
        lane = lax.broadcasted_iota(jnp.int32, (t, LANES), 1)
        rc = jnp.sum(jnp.where(lane == e, rcol_ref[...], 0.0), axis=-1, keepdims=True)
        ge = jnp.sum(jnp.where(lane == e, gate_ref[...], 0.0), axis=-1, keepdims=True)
        scatter(0, main_rows, rc, ge)
        lax.fori_loop(0, n_extra, lambda j, c: (scatter(extra_rows(j), MOE_EXTRA, rc, ge), c)[1], 0)

    @pl.when(jnp.logical_and(e == n_e - 1, f == n_f - 1))
    def _():
        o_ref[...] = x_ref[...] + mod_ref[0][5:6] * o_ref[...]


def _moe(h, gates, counts, wg, wu, wd, x, mods, tf, t, n_lat_rows, seq):
    m = h.shape[0]
    n_e, _, ff = wg.shape
    main_rows = 5 * t // 16
    cap = main_rows + -(-(t - main_rows) // MOE_EXTRA) * MOE_EXTRA
    midx = _mod_index(n_lat_rows, seq, t)
    grid_spec = pltpu.PrefetchScalarGridSpec(
        num_scalar_prefetch=1,
        grid=(m // t, n_e, ff // tf),
        in_specs=[pl.BlockSpec((t, D_MODEL), lambda i, e, f, c: (i, 0)),
                  pl.BlockSpec((t, LANES), lambda i, e, f, c: (i, 0)),
                  pl.BlockSpec((1, D_MODEL, tf), lambda i, e, f, c: (e, 0, f)),
                  pl.BlockSpec((1, D_MODEL, tf), lambda i, e, f, c: (e, 0, f)),
                  pl.BlockSpec((1, tf, D_MODEL), lambda i, e, f, c: (e, f, 0)),
                  pl.BlockSpec((t, D_MODEL), lambda i, e, f, c: (i, 0)),
                  pl.BlockSpec((1, 6, D_MODEL), lambda i, e, f, c: (midx(i), 0, 0))],
        out_specs=pl.BlockSpec((t, D_MODEL), lambda i, e, f, c: (i, 0)),
        scratch_shapes=[pltpu.VMEM((cap, D_MODEL), BF16), pltpu.VMEM((cap, D_MODEL), F32),
                        pltpu.VMEM((t, LANES), F32), pltpu.VMEM((N_EXPERTS, t), F32)],
    )
    return pl.pallas_call(
        functools.partial(_moe_kernel, main_rows=main_rows),
        grid_spec=grid_spec,
        out_shape=jax.ShapeDtypeStruct((m, D_MODEL), F32),
        compiler_params=_cparams(("parallel", "arbitrary", "arbitrary")),
        name="moe",
    )(counts, h, gates, wg, wu, wd, x, mods)


def _final_norm_kernel(x_ref, g_ref, o_ref):
    x = x_ref[...]
    o_ref[...] = x * lax.rsqrt(jnp.mean(x * x, axis=-1, keepdims=True) + EPS) * g_ref[...]


def _final_norm(x, gain, rows):
    tm = 512
    return pl.pallas_call(
        _final_norm_kernel,
        grid=(rows // tm,),
        in_specs=[pl.BlockSpec((tm, D_MODEL), lambda i: (i, 0)),
                  pl.BlockSpec((1, D_MODEL), lambda i: (0, 0))],
        out_specs=pl.BlockSpec((tm, D_MODEL), lambda i: (i, 0)),
        out_shape=jax.ShapeDtypeStruct((rows, D_MODEL), F32),
        compiler_params=_cparams(("parallel",)),
        name="final_norm",
    )(x, gain.reshape(1, D_MODEL))


def _split_w_in(w):
    sizes = (1536, 512, 8, 8, 1024, 512, 16, 512, 128, 128, 512, 512, 4096)
    offs = np.concatenate([[0], np.cumsum(sizes)])
    (qkv, og, a, b, xbc, z, dt, q, k, v, u, sv, gate) = [w[:, int(offs[i]):int(offs[i + 1])] for i in range(13)]
    main = jnp.concatenate([qkv, og, xbc, z, q, u, sv, gate, k, v], axis=1).astype(BF16)
    small = jnp.concatenate([a, b, dt, jnp.zeros((w.shape[0], LANES - SMALL_ROWS), w.dtype)], axis=1).astype(BF16)
    return main, small, small[:, :SMALL_ROWS].T


def _scan_params(gdn_a_log, gdn_dt_bias, ssd_a_log, ssd_dt_bias):
    zeros8 = jnp.zeros((8,), F32)
    alog = jnp.concatenate([gdn_a_log.reshape(-1), zeros8, ssd_a_log.reshape(-1)]).astype(F32)
    dtb = jnp.concatenate([gdn_dt_bias.reshape(-1), zeros8, ssd_dt_bias.reshape(-1)]).astype(F32)
    pad = jnp.zeros((LANES - SMALL_ROWS,), F32)
    pcol = jnp.zeros((8, LANES), F32).at[0].set(jnp.concatenate([alog, pad])).at[1].set(jnp.concatenate([dtb, pad]))
    prow_a = jnp.broadcast_to(alog[:, None], (SMALL_ROWS, SCAN_CHUNK))
    prow_b = jnp.broadcast_to(dtb[:, None], (SMALL_ROWS, SCAN_CHUNK))
    return pcol, prow_a, prow_b


def _rope_tables(seq, ctx_len):
    rows = seq // GRID_W
    row = jnp.repeat(jnp.arange(rows), GRID_W)
    col = jnp.tile(jnp.arange(GRID_W), rows)
    inv_freq = ROPE_BASE ** (-jnp.arange(ROPE_FREQS, dtype=F32) / ROPE_FREQS)
    ang = jnp.stack([row, col], axis=-1).astype(F32)[..., None] * inv_freq
    cos, sin = jnp.cos(ang), jnp.sin(ang)
    cos_h = jnp.concatenate([cos, cos], axis=-1).reshape(seq, ATT_HEAD_DIM)
    sin_h = jnp.concatenate([-sin, sin], axis=-1).reshape(seq, ATT_HEAD_DIM)
    cos_t = jnp.concatenate([jnp.tile(cos_h, (1, LANES // ATT_HEAD_DIM)), jnp.ones((ctx_len, LANES), F32)], axis=0)
    sin_t = jnp.concatenate([jnp.tile(sin_h, (1, LANES // ATT_HEAD_DIM)), jnp.zeros((ctx_len, LANES), F32)], axis=0)
    return cos_t, sin_t


def kernel(x, c, ctx, c_ctx, w_ada, b_ada, norm1, norm2, w_in, gdn_conv, gdn_A_log, gdn_dt_bias, gdn_norm,
           ssd_conv, ssd_conv_b, ssd_A_log, ssd_dt_bias, ssd_D, ssd_norm, attn_sink, mlp_ws, mlp_bs,
           w_branch, w_out, ffn_wg, ffn_wu, ffn_wd, moe_router, moe_wg, moe_wu, moe_wd, final_norm):
    bsz, seq, _ = x.shape
    ctx_len = ctx.shape[1]
    depth = w_ada.shape[0]
    n_lat = bsz * seq
    assert seq % 512 == 0 and ctx_len == CONV_TILE and (bsz * ctx_len) % 512 == 0

    xf = jnp.concatenate([x.reshape(n_lat, D_MODEL), ctx.reshape(bsz * ctx_len, D_MODEL)], axis=0)
    c_rows = jnp.zeros((16, D_MODEL), F32).at[0].set(c_ctx).at[1:1 + bsz].set(c)
    mods_all = _ada_mods(c_rows, w_ada, b_ada).reshape(depth, 16, 6, D_MODEL)
    cos_t, sin_t = _rope_tables(seq, ctx_len)
    moe_block = math.gcd(1024, seq, bsz * ctx_len)

    for i in range(depth):
        mods = mods_all[i]
        w_main, w_small, w_small_t = _split_w_in(w_in[i])
        pcol, prow_a, prow_b = _scan_params(gdn_A_log[i], gdn_dt_bias[i], ssd_A_log[i], ssd_dt_bias[i])

        h = _normmod(xf, norm1[i], mods, 0, n_lat, seq, BF16)
        proj = _matmul(h, w_main, BF16, 512, PROJ_MAIN // 2)
        small, small_t = _proj_small(h, w_small, w_small_t)
        small_t = small_t.reshape(SMALL_ROWS, -1, SCAN_CHUNK).transpose(1, 0, 2)

        qkv = _conv_act(proj, C_QKV, 3 * GDN_WIDTH, gdn_conv[i], jnp.zeros((3 * GDN_WIDTH,), F32), n_lat, seq, True)
        xbc = _conv_act(proj, C_XBC, SSD_CONV_CH, ssd_conv[i], ssd_conv_b[i], n_lat, seq, False)
        gdn_f, gdn_b = _gdn_mixer(qkv, small, small_t, pcol, prow_a, prow_b, bsz, seq, ctx_len)
        ssd_f, ssd_b = _ssd_scan(xbc, small, small_t, pcol, prow_a, prow_b, bsz, seq, ctx_len)
        att = _attention(proj, attn_sink[i].astype(F32), cos_t, sin_t, bsz, seq, ctx_len)
        bs_b = jnp.broadcast_to(mlp_bs[i][:, :, None], (MLP_GROUPS, MLP_CHUNK, MLP_GROUP_DIM)).astype(F32)
        mlp = _gmlp(proj, mlp_ws[i].astype(BF16), bs_b)

        xf = _merge(gdn_f, gdn_b, ssd_f, ssd_b, xbc, att, mlp, proj, xf, mods,
                    w_branch[i].astype(BF16), w_out[i].astype(BF16),
                    jnp.tile(gdn_norm[i], GDN_HEADS).reshape(1, GDN_WIDTH).astype(F32),
                    jnp.repeat(ssd_D[i], SSD_HEAD_DIM).reshape(1, SSD_WIDTH).astype(F32),
                    ssd_norm[i].reshape(1, SSD_WIDTH).astype(F32), n_lat, seq)

        j = i // 2
        if i % 2 == 0:
            h2 = _normmod(xf, norm2[i], mods, 3, n_lat, seq, BF16)
            xf = _ffn(h2, ffn_wg[j].astype(BF16), ffn_wu[j].astype(BF16), ffn_wd[j].astype(BF16),
                      xf, mods, 1408, n_lat, seq)
        else:
            router = jnp.concatenate([moe_router[j], jnp.zeros((D_MODEL, LANES - N_EXPERTS), F32)], axis=1)
            h2, gates, cnt = _normmod(xf, norm2[i], mods, 3, n_lat, seq, BF16, router=router)
            cnt = cnt[:, 0, :N_EXPERTS].reshape(-1, moe_block // ROUTER_TILE, N_EXPERTS).sum(axis=1)
            xf = _moe(h2, gates, cnt.astype(jnp.int32).reshape(-1), moe_wg[j].astype(BF16),
                      moe_wu[j].astype(BF16), moe_wd[j].astype(BF16), xf, mods, 896, moe_block, n_lat, seq)

    return _final_norm(xf, final_norm, n_lat).reshape(bsz, seq, D_MODEL)
```

```python
import functools
import math

import jax
import jax.numpy as jnp
import numpy as np
from jax import lax
from jax.experimental import pallas as pl
from jax.experimental.pallas import tpu as pltpu

F32 = jnp.float32
BF16 = jnp.bfloat16

D_MODEL = 1024
GRID_W = 64
EPS = 1e-6
N_DIR = 2

GDN_HEADS = 4
GDN_HEAD_DIM = 128
GDN_WIDTH = 512
SSD_HEADS = 8
SSD_HEAD_DIM = 64
SSD_WIDTH = 512
SSD_GROUPS = 2
SSD_STATE = 128
SSD_CONV_CH = 1024
ATT_HEADS = 8
ATT_KV_HEADS = 2
ATT_HEAD_DIM = 64
ATT_WIDTH = 512
ATT_KV_WIDTH = 128
WINDOW = 128
ATT_BLOCK = 128
ROPE_BASE = 10000.0
ROPE_FREQS = 16
MLP_GROUPS = 4
MLP_GROUP_DIM = 128
MLP_WIDTH = 512
MLP_CHUNK = 128
N_BRANCH = 4
BRANCH_WIDTH = 512
N_EXPERTS = 8
TOP_K = 2

SCAN_CHUNK = 64
LANES = 128
VMEM_LIMIT = 56 * 1024 * 1024

C_QKV, C_OG, C_XBC, C_Z, C_Q, C_U, C_SV, C_GATE, C_K, C_V = (
    0, 1536, 2048, 3072, 3584, 4096, 4608, 5120, 9216, 9344)
PROJ_MAIN = 9472
S_A, S_B, S_DT = 0, 8, 16
SMALL_ROWS = 32


def _cparams(sem):
    return pltpu.CompilerParams(dimension_semantics=sem, vmem_limit_bytes=VMEM_LIMIT)


def _bf(x):
    return x.astype(BF16)


def _dot(a, b):
    return jnp.dot(a, b, preferred_element_type=F32)


def _dot_nt(a, b):
    return lax.dot_general(a, b, (((1,), (1,)), ((), ())), preferred_element_type=F32)


def _dot_tn(a, b):
    return lax.dot_general(a, b, (((0,), (0,)), ((), ())), preferred_element_type=F32)


def _split(a):
    hi = a.astype(BF16)
    lo = (a - hi.astype(F32)).astype(BF16)
    return hi, lo


def _dot3(a, b, dot=_dot):
    ah, al = _split(a)
    bh, bl = _split(b)
    return dot(ah, bh) + (dot(al, bh) + dot(ah, bl))


def _silu(x):
    return x * (1.0 / (1.0 + jnp.exp(-x)))


def _sigmoid(x):
    return 1.0 / (1.0 + jnp.exp(-x))


def _softplus(x):
    return jnp.maximum(x, 0.0) + jnp.log1p(jnp.exp(-jnp.abs(x)))


def _gelu_tanh(x):
    return 0.5 * x * (1.0 + jnp.tanh(math.sqrt(2.0 / math.pi) * (x + 0.044715 * (x * x * x))))


def _ada_kernel(c_ref, w_ref, b_ref, o_ref):
    a = _silu(c_ref[...])
    o_ref[0] = _dot3(a, w_ref[0]) + b_ref[0]


def _ada_mods(c_rows, w_ada, b_ada):
    depth, _, n = w_ada.shape
    tn = 1536
    return pl.pallas_call(
        _ada_kernel,
        grid=(depth, n // tn),
        in_specs=[pl.BlockSpec((16, D_MODEL), lambda l, j: (0, 0)),
                  pl.BlockSpec((1, D_MODEL, tn), lambda l, j: (l, 0, j)),
                  pl.BlockSpec((1, 1, tn), lambda l, j: (l, 0, j))],
        out_specs=pl.BlockSpec((1, 16, tn), lambda l, j: (l, 0, j)),
        out_shape=jax.ShapeDtypeStruct((depth, 16, n), F32),
        compiler_params=_cparams(("parallel", "parallel")),
        name="ada_mods",
    )(c_rows, w_ada, b_ada.reshape(depth, 1, n))


def _normmod_kernel(x_ref, g_ref, mod_ref, h_ref, *, row0):
    x = x_ref[...]
    xn = x * lax.rsqrt(jnp.mean(x * x, axis=-1, keepdims=True) + EPS) * g_ref[...]
    mod = mod_ref[0]
    h = xn * (1.0 + mod[row0 + 1:row0 + 2]) + mod[row0:row0 + 1]
    h_ref[...] = h.astype(h_ref.dtype)


def _normmod_router_kernel(x_ref, g_ref, mod_ref, r_ref, h_ref, gate_ref, cnt_ref, *, row0):
    x = x_ref[...]
    xn = x * lax.rsqrt(jnp.mean(x * x, axis=-1, keepdims=True) + EPS) * g_ref[...]
    mod = mod_ref[0]
    h = xn * (1.0 + mod[row0 + 1:row0 + 2]) + mod[row0:row0 + 1]
    h_ref[...] = h.astype(h_ref.dtype)
    logits = _dot3(h, r_ref[...])
    lane = lax.broadcasted_iota(jnp.int32, logits.shape, 1).astype(F32)
    logits = jnp.where(lane < N_EXPERTS, logits, -jnp.inf)
    m1 = jnp.max(logits, axis=-1, keepdims=True)
    i1 = jnp.min(jnp.where(logits == m1, lane, float(LANES)), axis=-1, keepdims=True)
    rest = jnp.where(lane == i1, -jnp.inf, logits)
    m2 = jnp.max(rest, axis=-1, keepdims=True)
    i2 = jnp.min(jnp.where(rest == m2, lane, float(LANES)), axis=-1, keepdims=True)
    e = jnp.exp(m2 - m1)
    w1 = 1.0 / (1.0 + e)
    w2 = e / (1.0 + e)
    gate = jnp.where(lane == i1, w1, 0.0) + jnp.where(lane == i2, w2, 0.0)
    gate_ref[...] = gate
    cnt = jnp.sum((gate != 0.0).astype(F32), axis=0, keepdims=True)
    cnt_ref[0] = jnp.broadcast_to(cnt, (8, LANES))


def _mod_index(n_lat_rows, seq, tm):
    def idx(i):
        r = i * tm
        return jnp.where(r < n_lat_rows, 1 + r // seq, 0)
    return idx


ROUTER_TILE = 512


def _normmod(x, gain, mods, row0, n_lat_rows, seq, out_dtype, router=None):
    m = x.shape[0]
    tm = ROUTER_TILE
    midx = _mod_index(n_lat_rows, seq, tm)
    in_specs = [pl.BlockSpec((tm, D_MODEL), lambda i: (i, 0)),
                pl.BlockSpec((1, D_MODEL), lambda i: (0, 0)),
                pl.BlockSpec((1, 6, D_MODEL), lambda i: (midx(i), 0, 0))]
    if router is None:
        return pl.pallas_call(
            functools.partial(_normmod_kernel, row0=row0),
            grid=(m // tm,),
            in_specs=in_specs,
            out_specs=pl.BlockSpec((tm, D_MODEL), lambda i: (i, 0)),
            out_shape=jax.ShapeDtypeStruct((m, D_MODEL), out_dtype),
            compiler_params=_cparams(("parallel",)),
            name="normmod",
        )(x, gain.reshape(1, D_MODEL), mods)
    return pl.pallas_call(
        functools.partial(_normmod_router_kernel, row0=row0),
        grid=(m // tm,),
        in_specs=in_specs + [pl.BlockSpec((D_MODEL, LANES), lambda i: (0, 0))],
        out_specs=[pl.BlockSpec((tm, D_MODEL), lambda i: (i, 0)),
                   pl.BlockSpec((tm, LANES), lambda i: (i, 0)),
                   pl.BlockSpec((1, 8, LANES), lambda i: (i, 0, 0))],
        out_shape=[jax.ShapeDtypeStruct((m, D_MODEL), out_dtype),
                   jax.ShapeDtypeStruct((m, LANES), F32),
                   jax.ShapeDtypeStruct((m // tm, 8, LANES), F32)],
        compiler_params=_cparams(("parallel",)),
        name="normmod_router",
    )(x, gain.reshape(1, D_MODEL), mods, router)


def _mm_kernel(a_ref, b_ref, o_ref):
    o_ref[...] = _dot(a_ref[...], b_ref[...]).astype(o_ref.dtype)


def _matmul(a, b, out_dtype, tm, tn):
    m, k = a.shape
    n = b.shape[1]
    return pl.pallas_call(
        _mm_kernel,
        grid=(n // tn, m // tm),
        in_specs=[pl.BlockSpec((tm, k), lambda j, i: (i, 0)),
                  pl.BlockSpec((k, tn), lambda j, i: (0, j))],
        out_specs=pl.BlockSpec((tm, tn), lambda j, i: (i, j)),
        out_shape=jax.ShapeDtypeStruct((m, n), out_dtype),
        compiler_params=_cparams(("parallel", "parallel")),
        name="proj_main",
    )(a, b)


def _proj_small_kernel(a_ref, w_ref, wt_ref, o_ref, ot_ref):
    a = a_ref[...]
    o_ref[...] = _dot(a, w_ref[...])
    ot_ref[...] = _dot_nt(wt_ref[...], a)


def _proj_small(h, w_small, w_small_t):
    m = h.shape[0]
    tm = 512
    return pl.pallas_call(
        _proj_small_kernel,
        grid=(m // tm,),
        in_specs=[pl.BlockSpec((tm, D_MODEL), lambda i: (i, 0)),
                  pl.BlockSpec((D_MODEL, LANES), lambda i: (0, 0)),
                  pl.BlockSpec((SMALL_ROWS, D_MODEL), lambda i: (0, 0))],
        out_specs=[pl.BlockSpec((tm, LANES), lambda i: (i, 0)),
                   pl.BlockSpec((SMALL_ROWS, tm), lambda i: (0, i))],
        out_shape=[jax.ShapeDtypeStruct((m, LANES), F32),
                   jax.ShapeDtypeStruct((SMALL_ROWS, m), F32)],
        compiler_params=_cparams(("parallel",)),
        name="proj_small",
    )(h, w_small, w_small_t)


CONV_TILE = 256
HALO = 16


def _conv_kernel(x_ref, prev_ref, next_ref, w_ref, b_ref, o_ref, *, tiles_per_seq, n_lat_tiles, l2_heads):
    j = pl.program_id(0)
    is_lat = j < n_lat_tiles
    first = jnp.logical_or(jnp.logical_not(is_lat), j % tiles_per_seq == 0)
    last = jnp.logical_or(jnp.logical_not(is_lat), j % tiles_per_seq == tiles_per_seq - 1)
    x = x_ref[...].astype(F32)
    rows = x.shape[0]
    prow = jnp.where(first, 0.0, prev_ref[...].astype(F32)[HALO - 1:HALO])
    nrow = jnp.where(last, 0.0, next_ref[...].astype(F32)[0:1])
    ridx = lax.broadcasted_iota(jnp.int32, x.shape, 0)
    xp = jnp.where(ridx == 0, prow, pltpu.roll(x, 1, axis=0))
    xn = jnp.where(ridx == rows - 1, nrow, pltpu.roll(x, rows - 1, axis=0))
    w = w_ref[...]
    y = _silu(w[0:1] * xp + w[1:2] * x + w[2:3] * xn + b_ref[...])
    if l2_heads:
        parts = []
        for hh in range(2 * GDN_HEADS):
            seg = y[:, hh * GDN_HEAD_DIM:(hh + 1) * GDN_HEAD_DIM]
            inv = lax.rsqrt(jnp.sum(seg * seg, axis=-1, keepdims=True) + EPS)
            if hh < GDN_HEADS:
                inv = inv * (GDN_HEAD_DIM ** -0.5)
            parts.append(seg * inv)
        parts.append(y[:, 2 * GDN_WIDTH:])
        y = jnp.concatenate(parts, axis=1)
    o_ref[...] = y.astype(o_ref.dtype)


def _conv_act(proj, col0, width, w, b, n_lat_rows, seq, l2_heads):
    m = proj.shape[0]
    n_tiles = m // CONV_TILE
    per_tile = CONV_TILE // HALO
    n_halo = m // HALO
    cb = col0 // width
    assert col0 % width == 0
    return pl.pallas_call(
        functools.partial(_conv_kernel, tiles_per_seq=seq // CONV_TILE,
                          n_lat_tiles=n_lat_rows // CONV_TILE, l2_heads=l2_heads),
        grid=(n_tiles,),
        in_specs=[pl.BlockSpec((CONV_TILE, width), lambda j: (j, cb)),
                  pl.BlockSpec((HALO, width), lambda j: (jnp.maximum(j * per_tile - 1, 0), cb)),
                  pl.BlockSpec((HALO, width), lambda j: (jnp.minimum((j + 1) * per_tile, n_halo - 1), cb)),
                  pl.BlockSpec((3, width), lambda j: (0, 0)),
                  pl.BlockSpec((1, width), lambda j: (0, 0))],
        out_specs=pl.BlockSpec((CONV_TILE, width), lambda j: (j, 0)),
        out_shape=jax.ShapeDtypeStruct((m, width), BF16),
        compiler_params=_cparams(("parallel",)),
        name="conv_act",
    )(proj, proj, proj, w, b.reshape(1, width))


def _chunk_maps(bsz, seq, ctx_len, cs):
    nc_ctx, nc_lat = ctx_len // cs, seq // cs
    lat_blocks = bsz * nc_lat

    def fwd(b, t):
        return jnp.where(t < nc_ctx, lat_blocks + b * nc_ctx + t, b * nc_lat + (t - nc_ctx))

    def bwd(b, t):
        return jnp.where(t < nc_ctx, lat_blocks + b * nc_ctx + (nc_ctx - 1 - t),
                         b * nc_lat + (nc_lat - 1 - (t - nc_ctx)))

    return fwd, bwd, nc_ctx + nc_lat


def _tri_masks(c):
    ri = lax.broadcasted_iota(jnp.int32, (c, c), 0)
    ci = lax.broadcasted_iota(jnp.int32, (c, c), 1)
    return (ri >= ci, ri <= ci), (ri > ci, ri < ci), ri == ci


def _bdot(a, b):
    return lax.dot_general(a, b, (((2,), (1,)), ((0,), (0,))), preferred_element_type=F32)


def _bdot1(a, b):
    return _bdot(_bf(a), _bf(b))


def _unit_tri_solve(nmat, rhs):
    c = nmat.shape[1]
    ri = lax.broadcasted_iota(jnp.int32, (c, c), 0)
    ci = lax.broadcasted_iota(jnp.int32, (c, c), 1)
    same = lambda size: ((ri // size) == (ci // size))[None]
    base = 16
    m = jnp.where(same(base), nmat, 0.0)
    t = (ri == ci).astype(F32)[None] - m
    k = 2
    while k < base:
        m = _bdot1(m, m)
        t = t + _bdot1(t, m)
        k *= 2
    size = base
    while size < c:
        off = jnp.where(jnp.logical_and(same(2 * size), jnp.logical_not(same(size))), nmat, 0.0)
        t = t - _bdot1(_bdot1(t, off), t)
        size *= 2
    x0 = _bdot1(t, rhs)
    nh, nl = _split(nmat)
    xh, xl = _split(x0)
    resid = rhs - x0 - (_bdot(nh, xh) + (_bdot(nl, xh) + _bdot(nh, xl)))
    return x0 + _bdot1(t, resid)


GDN_LOCAL_CHUNKS = 2


def _gdn_local_kernel(qkv_ref, sm_ref, smt_ref, pcol_ref, prow_a_ref, prow_b_ref,
                      a1_ref, u_ref, a2_ref, eg_ref):
    c = SCAN_CHUNK
    incl, strict, _ = _tri_masks(c)
    tri = incl[0].astype(F32)
    eye_k = (lax.broadcasted_iota(jnp.int32, (GDN_HEAD_DIM, GDN_HEAD_DIM), 0)
             == lax.broadcasted_iota(jnp.int32, (GDN_HEAD_DIM, GDN_HEAD_DIM), 1)).astype(BF16)
    pcol = pcol_ref[...]
    alog_row, dtb_row = pcol[0:1], pcol[1:2]
    nmats, rhss, rest = [], [], []
    for ck in range(GDN_LOCAL_CHUNKS):
        rows = slice(ck * c, (ck + 1) * c)
        sm = sm_ref[rows, :]
        smt = smt_ref[ck]
        g_cols = -jnp.exp(alog_row) * _softplus(sm + dtb_row)
        beta_cols = _sigmoid(sm)
        g_rows = -jnp.exp(prow_a_ref[...]) * _softplus(smt + prow_b_ref[...])
        pre_cols = _dot3(tri, g_cols)
        pre_rows = _dot3(g_rows, tri, dot=_dot_nt)
        tot_cols, tot_rows = pre_cols[c - 1:c, :], pre_rows[:, c - 1:c]
        gcum_cols = (pre_cols, tot_cols - pre_cols + g_cols)
        gcum_rows = (pre_rows, tot_rows - pre_rows + g_rows)
        for hh in range(GDN_HEADS):
            q = qkv_ref[rows, hh * GDN_HEAD_DIM:(hh + 1) * GDN_HEAD_DIM]
            k = qkv_ref[rows, GDN_WIDTH + hh * GDN_HEAD_DIM:GDN_WIDTH + (hh + 1) * GDN_HEAD_DIM]
            v = qkv_ref[rows, 2 * GDN_WIDTH + hh * GDN_HEAD_DIM:2 * GDN_WIDTH + (hh + 1) * GDN_HEAD_DIM]
            kk = _dot_nt(k, k)
            qk = _dot_nt(q, k)
            kt = _dot_nt(eye_k, k)
            qf, kf, vf = q.astype(F32), k.astype(F32), v.astype(F32)
            for d in range(N_DIR):
                col = S_A + d * GDN_HEADS + hh
                gc = gcum_cols[d][:, col:col + 1]
                gr = gcum_rows[d][col:col + 1, :]
                g_last = tot_rows[col:col + 1, :]
                decay = jnp.where(incl[d], jnp.exp(jnp.minimum(gc - gr, 0.0)), 0.0)
                beta = beta_cols[:, S_B + d * GDN_HEADS + hh:S_B + d * GDN_HEADS + hh + 1]
                eg = jnp.exp(gc)
                nmats.append(jnp.where(strict[d], kk * decay, 0.0) * beta)
                rhss.append(jnp.concatenate([vf * beta, kf * (beta * eg)], axis=1))
                rest.append((d, ck, hh, qf * eg, qk * decay, kt * jnp.exp(g_last - gr), jnp.exp(g_last)))
    sol = _unit_tri_solve(jnp.stack(nmats), jnp.stack(rhss))
    for idx, (d, ck, hh, q_in, qkm, kout_t, eg_last) in enumerate(rest):
        u_ref[d, ck, hh] = _bf(sol[idx, :, :GDN_HEAD_DIM])
        a1_ref[d, ck, hh, 0:c, :] = _bf(sol[idx, :, GDN_HEAD_DIM:])
        a1_ref[d, ck, hh, c:2 * c, :] = _bf(q_in)
        a2_ref[d, ck, hh, 0:c, :] = _bf(qkm)
        a2_ref[d, ck, hh, c:c + GDN_HEAD_DIM, :] = _bf(kout_t)
        row = d * GDN_HEADS + hh
        eg_ref[ck, row:row + 1, :] = jnp.broadcast_to(eg_last, (1, LANES))


def _gdn_scan_kernel(a1f_ref, a1b_ref, uf_ref, ub_ref, a2f_ref, a2b_ref, egf_ref, egb_ref,
                     of_ref, ob_ref, state_ref):
    t = pl.program_id(1)

    @pl.when(t == 0)
    def _():
        state_ref[...] = jnp.zeros_like(state_ref)

    c = SCAN_CHUNK
    dirs = ((a1f_ref, uf_ref, a2f_ref, egf_ref, of_ref), (a1b_ref, ub_ref, a2b_ref, egb_ref, ob_ref))
    chains = [(d, hh) for d in range(N_DIR) for hh in range(GDN_HEADS)]
    states = {ch: state_ref[ch[0], ch[1]] for ch in chains}
    m1 = {(d, hh): _dot(dirs[d][0][0, 0, hh], _bf(states[d, hh])) for d, hh in chains}
    m2 = {(d, hh): _dot(dirs[d][2][0, 0, hh], _bf(dirs[d][1][0, 0, hh].astype(F32) - m1[d, hh][0:c]))
          for d, hh in chains}
    for d, hh in chains:
        row = d * GDN_HEADS + hh
        dirs[d][4][:, hh * GDN_HEAD_DIM:(hh + 1) * GDN_HEAD_DIM] = m1[d, hh][c:2 * c] + m2[d, hh][0:c]
        state_ref[d, hh] = states[d, hh] * dirs[d][3][0, row:row + 1, :] + m2[d, hh][c:c + GDN_HEAD_DIM]


def _gdn_mixer(qkv, small, small_t, pcol, prow_a, prow_b, bsz, seq, ctx_len):
    m = qkv.shape[0]
    c = SCAN_CHUNK
    n_chunks = m // c
    cb = GDN_LOCAL_CHUNKS
    hd = GDN_HEAD_DIM
    const = lambda shape: pl.BlockSpec(shape, lambda i: (0,) * len(shape))
    a1, u, a2, eg = pl.pallas_call(
        _gdn_local_kernel,
        grid=(n_chunks // cb,),
        in_specs=[pl.BlockSpec((cb * c, 3 * GDN_WIDTH), lambda i: (i, 0)),
                  pl.BlockSpec((cb * c, LANES), lambda i: (i, 0)),
                  pl.BlockSpec((cb, SMALL_ROWS, c), lambda i: (i, 0, 0)),
                  const((8, LANES)), const((SMALL_ROWS, c)), const((SMALL_ROWS, c))],
        out_specs=[pl.BlockSpec((N_DIR, cb, GDN_HEADS, 2 * c, hd), lambda i: (0, i, 0, 0, 0)),
                   pl.BlockSpec((N_DIR, cb, GDN_HEADS, c, hd), lambda i: (0, i, 0, 0, 0)),
                   pl.BlockSpec((N_DIR, cb, GDN_HEADS, c + hd, c), lambda i: (0, i, 0, 0, 0)),
                   pl.BlockSpec((cb, 8, LANES), lambda i: (i, 0, 0))],
        out_shape=[jax.ShapeDtypeStruct((N_DIR, n_chunks, GDN_HEADS, 2 * c, hd), BF16),
                   jax.ShapeDtypeStruct((N_DIR, n_chunks, GDN_HEADS, c, hd), BF16),
                   jax.ShapeDtypeStruct((N_DIR, n_chunks, GDN_HEADS, c + hd, c), BF16),
                   jax.ShapeDtypeStruct((n_chunks, 8, LANES), F32)],
        compiler_params=_cparams(("parallel",)),
        name="gdn_local",
    )(qkv, small, small_t, pcol, prow_a, prow_b)

    fwd, bwd, steps = _chunk_maps(bsz, seq, ctx_len, c)
    per_dir = lambda rows, cols: [pl.BlockSpec((1, 1, GDN_HEADS, rows, cols), lambda b, t: (0, fwd(b, t), 0, 0, 0)),
                                  pl.BlockSpec((1, 1, GDN_HEADS, rows, cols), lambda b, t: (1, bwd(b, t), 0, 0, 0))]
    out = lambda f: pl.BlockSpec((c, GDN_WIDTH), lambda b, t: (f(b, t), 0))
    return pl.pallas_call(
        _gdn_scan_kernel,
        grid=(bsz, steps),
        in_specs=(per_dir(2 * c, hd) + per_dir(c, hd) + per_dir(c + hd, c)
                  + [pl.BlockSpec((1, 8, LANES), lambda b, t: (fwd(b, t), 0, 0)),
                     pl.BlockSpec((1, 8, LANES), lambda b, t: (bwd(b, t), 0, 0))]),
        out_specs=[out(fwd), out(bwd)],
        out_shape=[jax.ShapeDtypeStruct((m, GDN_WIDTH), F32)] * 2,
        scratch_shapes=[pltpu.VMEM((N_DIR, GDN_HEADS, hd, hd), F32)],
        compiler_params=_cparams(("parallel", "arbitrary")),
        name="gdn_scan",
    )(a1, a1, u, u, a2, a2, eg, eg)


def _lane_expand(cols, width):
    rows = cols[0].shape[0]
    lane = lax.broadcasted_iota(jnp.int32, (rows, len(cols) * width), 1) // width
    out = jnp.broadcast_to(cols[-1], lane.shape)
    for r in range(len(cols) - 2, -1, -1):
        out = jnp.where(lane == r, cols[r], out)
    return out


def _ssd_kernel(xbc_f_ref, xbc_b_ref, sm_f_ref, sm_b_ref, smt_f_ref, smt_b_ref,
                pcol_ref, prow_a_ref, prow_b_ref, yf_ref, yb_ref, state_ref):
    t = pl.program_id(1)

    @pl.when(t == 0)
    def _():
        state_ref[...] = jnp.zeros_like(state_ref)

    c = SCAN_CHUNK
    p = SSD_HEAD_DIM
    hpg = SSD_HEADS // SSD_GROUPS
    gw = hpg * p
    incl, _, _ = _tri_masks(c)
    tri = incl[0].astype(F32)
    eye_n = (lax.broadcasted_iota(jnp.int32, (SSD_STATE, SSD_STATE), 0)
             == lax.broadcasted_iota(jnp.int32, (SSD_STATE, SSD_STATE), 1)).astype(BF16)
    pcol = pcol_ref[...]
    alog_row, dtb_row = pcol[0:1], pcol[1:2]
    dirs = ((xbc_f_ref, sm_f_ref, smt_f_ref, yf_ref), (xbc_b_ref, sm_b_ref, smt_b_ref, yb_ref))
    lmats, xdts, per = [], [], []
    for d, (xbc_ref, sm_ref, smt_ref, y_ref) in enumerate(dirs):
        dt_cols = _softplus(sm_ref[...] + dtb_row)
        a_cols = -jnp.exp(alog_row) * dt_cols
        a_rows = -jnp.exp(prow_a_ref[...]) * _softplus(smt_ref[0] + prow_b_ref[...])
        pre_cols = _dot3(tri, a_cols)
        pre_rows = _dot3(a_rows, tri, dot=_dot_nt)
        tot_cols, tot_rows = pre_cols[c - 1:c, :], pre_rows[:, c - 1:c]
        if d == 0:
            acs_cols, acs_rows = pre_cols, pre_rows
        else:
            acs_cols, acs_rows = tot_cols - pre_cols + a_cols, tot_rows - pre_rows + a_rows
        for g in range(SSD_GROUPS):
            bm = xbc_ref[:, SSD_WIDTH + g * SSD_STATE:SSD_WIDTH + (g + 1) * SSD_STATE]
            cm = xbc_ref[:, SSD_WIDTH + (SSD_GROUPS + g) * SSD_STATE:SSD_WIDTH + (SSD_GROUPS + g + 1) * SSD_STATE]
            cb = _dot_nt(cm, bm)
            bm_t = _bf(_dot_nt(eye_n, bm))
            cols = [S_DT + d * SSD_HEADS + g * hpg + r for r in range(hpg)]
            acs = [acs_cols[:, col:col + 1] for col in cols]
            a_last = [tot_rows[col:col + 1, :] for col in cols]
            xdt = xbc_ref[:, g * gw:(g + 1) * gw].astype(F32) * _lane_expand(
                [dt_cols[:, col:col + 1] for col in cols], p)
            for r, col in enumerate(cols):
                ar = acs_rows[col:col + 1, :]
                lmats.append(cb * jnp.where(incl[d], jnp.exp(jnp.minimum(acs[r] - ar, 0.0)), 0.0))
                xdts.append(xdt[:, r * p:(r + 1) * p])
            decay_out = _lane_expand([jnp.exp(al - ac) for al, ac in zip(a_last, acs)], p)
            cs = _dot(bm_t, _bf(xdt * decay_out))
            s = state_ref[d, g]
            y_off = _dot(cm, _bf(s)) * _lane_expand([jnp.exp(ac) for ac in acs], p)
            s_new = s * _lane_expand([jnp.exp(al) for al in a_last], p) + cs
            per.append((d, g, y_ref, y_off, s_new))
    y_diag = _bdot1(jnp.stack(lmats), jnp.stack(xdts))
    for idx, (d, g, y_ref, y_off, s_new) in enumerate(per):
        for r in range(hpg):
            lo = g * gw + r * p
            y_ref[:, lo:lo + p] = y_diag[idx * hpg + r] + y_off[:, r * p:(r + 1) * p]
        state_ref[d, g] = s_new


def _ssd_scan(xbc, small, small_t, pcol, prow_a, prow_b, bsz, seq, ctx_len):
    m = xbc.shape[0]
    c = SCAN_CHUNK
    fwd, bwd, steps = _chunk_maps(bsz, seq, ctx_len, c)
    spec = lambda width, f: pl.BlockSpec((c, width), lambda b, t: (f(b, t), 0))
    spec_t = lambda f: pl.BlockSpec((1, SMALL_ROWS, c), lambda b, t: (f(b, t), 0, 0))
    const = lambda shape: pl.BlockSpec(shape, lambda b, t: (0,) * len(shape))
    return pl.pallas_call(
        _ssd_kernel,
        grid=(bsz, steps),
        in_specs=[spec(SSD_CONV_CH, fwd), spec(SSD_CONV_CH, bwd), spec(LANES, fwd), spec(LANES, bwd),
                  spec_t(fwd), spec_t(bwd), const((8, LANES)), const((SMALL_ROWS, c)), const((SMALL_ROWS, c))],
        out_specs=[spec(SSD_WIDTH, fwd), spec(SSD_WIDTH, bwd)],
        out_shape=[jax.ShapeDtypeStruct((m, SSD_WIDTH), F32)] * 2,
        scratch_shapes=[pltpu.VMEM((N_DIR, SSD_GROUPS, SSD_STATE, SSD_WIDTH // SSD_GROUPS), F32)],
        compiler_params=_cparams(("parallel", "arbitrary")),
        name="ssd_scan",
    )(xbc, xbc, small, small, small_t, small_t, pcol, prow_a, prow_b)


def _rope(t, cos, sin_signed):
    width = t.shape[1]
    lane = lax.broadcasted_iota(jnp.int32, t.shape, 1)
    partner = jnp.where(lane % 32 < ROPE_FREQS, pltpu.roll(t, width - ROPE_FREQS, axis=1),
                        pltpu.roll(t, ROPE_FREQS, axis=1))
    return t * cos + partner * sin_signed


def _attn_kernel(sink_ref, q_ref, k0_ref, k1_ref, k2_ref, v0_ref, v1_ref, v2_ref, kc_ref, vc_ref,
                 cq_ref, sq_ref, c0_ref, c1_ref, c2_ref, s0_ref, s1_ref, s2_ref, o_ref, *, n_lat_blocks):
    i = pl.program_id(1)
    blk = ATT_BLOCK
    is_lat = i < n_lat_blocks
    rep = ATT_HEADS // ATT_KV_HEADS
    cq = jnp.concatenate([cq_ref[...]] * (ATT_WIDTH // LANES), axis=1)
    sq = jnp.concatenate([sq_ref[...]] * (ATT_WIDTH // LANES), axis=1)
    q = _rope(q_ref[...].astype(F32), cq, sq) * (ATT_HEAD_DIM ** -0.5)
    kwin = jnp.concatenate([_rope(k0_ref[...].astype(F32), c0_ref[...], s0_ref[...]),
                            _rope(k1_ref[...].astype(F32), c1_ref[...], s1_ref[...]),
                            _rope(k2_ref[...].astype(F32), c2_ref[...], s2_ref[...])], axis=0)
    vwin = jnp.concatenate([v0_ref[...], v1_ref[...], v2_ref[...]], axis=0)
    kc = kc_ref[...]
    vc = vc_ref[...]
    rows = rep * blk
    qpos = i * blk + lax.broadcasted_iota(jnp.int32, (rows, 3 * blk), 0) % blk
    kpos = (i - 1) * blk + lax.broadcasted_iota(jnp.int32, (rows, 3 * blk), 1)
    valid = (jnp.abs(qpos - kpos) <= WINDOW) & (kpos >= 0) & (kpos < n_lat_blocks * blk) & is_lat
    rgrp = lax.broadcasted_iota(jnp.int32, (rows, 1), 0) // blk
    for g in range(ATT_KV_HEADS):
        qg = q[:, g * rep * ATT_HEAD_DIM:(g + 1) * rep * ATT_HEAD_DIM]
        qs = _bf(jnp.concatenate([qg[:, r * ATT_HEAD_DIM:(r + 1) * ATT_HEAD_DIM] for r in range(rep)], axis=0))
        kg = _bf(kwin[:, g * ATT_HEAD_DIM:(g + 1) * ATT_HEAD_DIM])
        vg = vwin[:, g * ATT_HEAD_DIM:(g + 1) * ATT_HEAD_DIM]
        s_loc = jnp.where(valid, _dot_nt(qs, kg), -jnp.inf)
        s_ctx = _dot_nt(qs, kc[:, g * ATT_HEAD_DIM:(g + 1) * ATT_HEAD_DIM])
        sink = jnp.zeros((rows, 1), F32)
        for r in range(rep):
            sink = jnp.where(rgrp == r, sink_ref[g * rep + r], sink)
        mx = jnp.maximum(jnp.maximum(jnp.max(s_loc, axis=-1, keepdims=True),
                                     jnp.max(s_ctx, axis=-1, keepdims=True)), sink)
        p_loc = jnp.exp(s_loc - mx)
        p_ctx = jnp.exp(s_ctx - mx)
        den = (jnp.sum(p_loc, axis=-1, keepdims=True) + jnp.sum(p_ctx, axis=-1, keepdims=True)
               + jnp.exp(sink - mx))
        o = (_dot(_bf(p_loc), vg) + _dot(_bf(p_ctx), vc[:, g * ATT_HEAD_DIM:(g + 1) * ATT_HEAD_DIM])) / den
        for r in range(rep):
            hcol = (g * rep + r) * ATT_HEAD_DIM
            o_ref[:, hcol:hcol + ATT_HEAD_DIM] = o[r * blk:(r + 1) * blk].astype(o_ref.dtype)


def _attention(proj, sink, cos_t, sin_t, bsz, seq, ctx_len):
    m = proj.shape[0]
    blk = ATT_BLOCK
    nl, ncx = seq // blk, ctx_len // blk
    lat_blocks = bsz * nl

    def qrow(b, i):
        return jnp.where(i < nl, b * nl + i, lat_blocks + b * ncx + (i - nl))

    def krow(off):
        return lambda b, i, s: (b * nl + jnp.clip(i + off, 0, nl - 1), C_K // ATT_KV_WIDTH)

    def vrow(off):
        return lambda b, i, s: (b * nl + jnp.clip(i + off, 0, nl - 1), C_V // ATT_KV_WIDTH)

    def trow(off):
        return lambda b, i, s: (jnp.clip(i + off, 0, nl - 1), 0)

    kv_spec = lambda f: pl.BlockSpec((blk, ATT_KV_WIDTH), f)
    tab = lambda f: pl.BlockSpec((blk, LANES), f)
    ctx_blk = lambda colblk: pl.BlockSpec((ctx_len, ATT_KV_WIDTH),
                                          lambda b, i, s: (bsz * seq // ctx_len + b, colblk))
    grid_spec = pltpu.PrefetchScalarGridSpec(
        num_scalar_prefetch=1,
        grid=(bsz, nl + ncx),
        in_specs=[pl.BlockSpec((blk, ATT_WIDTH), lambda b, i, s: (qrow(b, i), C_Q // ATT_WIDTH)),
                  kv_spec(krow(-1)), kv_spec(krow(0)), kv_spec(krow(1)),
                  kv_spec(vrow(-1)), kv_spec(vrow(0)), kv_spec(vrow(1)),
                  ctx_blk(C_K // ATT_KV_WIDTH), ctx_blk(C_V // ATT_KV_WIDTH),
                  tab(lambda b, i, s: (i, 0)), tab(lambda b, i, s: (i, 0)),
                  tab(trow(-1)), tab(trow(0)), tab(trow(1)),
                  tab(trow(-1)), tab(trow(0)), tab(trow(1))],
        out_specs=pl.BlockSpec((blk, ATT_WIDTH), lambda b, i, s: (qrow(b, i), 0)),
    )
    return pl.pallas_call(
        functools.partial(_attn_kernel, n_lat_blocks=nl),
        grid_spec=grid_spec,
        out_shape=jax.ShapeDtypeStruct((m, ATT_WIDTH), BF16),
        compiler_params=_cparams(("parallel", "parallel")),
        name="window_attn",
    )(sink, proj, proj, proj, proj, proj, proj, proj, proj, proj,
      cos_t, sin_t, cos_t, cos_t, cos_t, sin_t, sin_t, sin_t)


GMLP_CHUNKS = 2


def _gmlp_kernel(u_ref, sv_ref, ws_ref, bs_ref, o_ref):
    for ck in range(GMLP_CHUNKS):
        rows = slice(ck * MLP_CHUNK, (ck + 1) * MLP_CHUNK)
        u = _gelu_tanh(u_ref[rows, :].astype(F32))
        v = _gelu_tanh(sv_ref[rows, :].astype(F32))
        for g in range(MLP_GROUPS):
            vg = v[:, g * MLP_GROUP_DIM:(g + 1) * MLP_GROUP_DIM]
            mu = jnp.mean(vg, axis=-1, keepdims=True)
            vc = vg - mu
            vn = vc * lax.rsqrt(jnp.mean(vc * vc, axis=-1, keepdims=True) + EPS)
            sp = _dot(ws_ref[g], _bf(vn)) + bs_ref[g]
            o_ref[rows, g * MLP_GROUP_DIM:(g + 1) * MLP_GROUP_DIM] = (
                u[:, g * MLP_GROUP_DIM:(g + 1) * MLP_GROUP_DIM] * sp).astype(o_ref.dtype)


def _gmlp(proj, ws, bs_b):
    m = proj.shape[0]
    ck = MLP_CHUNK
    tm = GMLP_CHUNKS * ck
    return pl.pallas_call(
        _gmlp_kernel,
        grid=(m // tm,),
        in_specs=[pl.BlockSpec((tm, MLP_WIDTH), lambda i: (i, C_U // MLP_WIDTH)),
                  pl.BlockSpec((tm, MLP_WIDTH), lambda i: (i, C_SV // MLP_WIDTH)),
                  pl.BlockSpec((MLP_GROUPS, ck, ck), lambda i: (0, 0, 0)),
                  pl.BlockSpec((MLP_GROUPS, ck, MLP_GROUP_DIM), lambda i: (0, 0, 0))],
        out_specs=pl.BlockSpec((tm, MLP_WIDTH), lambda i: (i, 0)),
        out_shape=jax.ShapeDtypeStruct((m, MLP_WIDTH), BF16),
        compiler_params=_cparams(("parallel",)),
        name="gmlp",
    )(proj, proj, ws, bs_b)


def _group_rms(y, gain, group):
    parts = []
    for s in range(0, y.shape[1], group):
        seg = y[:, s:s + group]
        parts.append(seg * lax.rsqrt(jnp.mean(seg * seg, axis=-1, keepdims=True) + EPS))
    return jnp.concatenate(parts, axis=1) * gain


def _merge_kernel(gf_ref, gb_ref, og_ref, sf_ref, sb_ref, xc_ref, z_ref, att_ref, mlp_ref,
                  g0_ref, g1_ref, g2_ref, g3_ref, x_ref, mod_ref, wb_ref, wo_ref,
                  gn_ref, dsk_ref, sn_ref, o_ref):
    ya = _group_rms(gf_ref[...] + gb_ref[...], gn_ref[...], GDN_HEAD_DIM) * _silu(og_ref[...].astype(F32))
    yb = sf_ref[...] + sb_ref[...] + dsk_ref[...] * xc_ref[...].astype(F32)
    yb = _group_rms(yb * _silu(z_ref[...].astype(F32)), sn_ref[...], SSD_WIDTH // SSD_GROUPS)
    ys = (_bf(ya), _bf(yb), att_ref[...], mlp_ref[...])
    acc = None
    for mi, (y, g_ref) in enumerate(zip(ys, (g0_ref, g1_ref, g2_ref, g3_ref))):
        term = _sigmoid(g_ref[...].astype(F32)) * _dot(y, wb_ref[mi])
        acc = term if acc is None else acc + term
    out = _dot(_bf(acc), wo_ref[...])
    o_ref[...] = x_ref[...] + mod_ref[0][2:3] * out


def _merge(gdn_f, gdn_b, ssd_f, ssd_b, xbc, att, mlp, proj, x, mods, wb, wo, gn, dsk, sn, n_lat_rows, seq):
    m = x.shape[0]
    tm = 256
    midx = _mod_index(n_lat_rows, seq, tm)
    half = lambda cb=0: pl.BlockSpec((tm, BRANCH_WIDTH), lambda i: (i, cb))
    gate = lambda k: pl.BlockSpec((tm, D_MODEL), lambda i: (i, C_GATE // D_MODEL + k))
    row = lambda w: pl.BlockSpec((1, w), lambda i: (0, 0))
    return pl.pallas_call(
        _merge_kernel,
        grid=(m // tm,),
        in_specs=[half(), half(), half(C_OG // BRANCH_WIDTH), half(), half(), half(0), half(C_Z // BRANCH_WIDTH),
                  half(), half(), gate(0), gate(1), gate(2), gate(3),
                  pl.BlockSpec((tm, D_MODEL), lambda i: (i, 0)),
                  pl.BlockSpec((1, 6, D_MODEL), lambda i: (midx(i), 0, 0)),
                  pl.BlockSpec((N_BRANCH, BRANCH_WIDTH, D_MODEL), lambda i: (0, 0, 0)),
                  pl.BlockSpec((D_MODEL, D_MODEL), lambda i: (0, 0)),
                  row(BRANCH_WIDTH), row(BRANCH_WIDTH), row(BRANCH_WIDTH)],
        out_specs=pl.BlockSpec((tm, D_MODEL), lambda i: (i, 0)),
        out_shape=jax.ShapeDtypeStruct((m, D_MODEL), F32),
        compiler_params=_cparams(("parallel",)),
        name="merge",
    )(gdn_f, gdn_b, proj, ssd_f, ssd_b, xbc, proj, att, mlp, proj, proj, proj, proj, x, mods, wb, wo, gn, dsk, sn)


def _ffn_kernel(h_ref, wg_ref, wu_ref, wd_ref, x_ref, mod_ref, o_ref, acc_ref):
    f = pl.program_id(1)

    @pl.when(f == 0)
    def _():
        acc_ref[...] = jnp.zeros_like(acc_ref)

    h = h_ref[...]
    a = _silu(_dot(h, wg_ref[...])) * _dot(h, wu_ref[...])
    acc_ref[...] += _dot(_bf(a), wd_ref[...])

    @pl.when(f == pl.num_programs(1) - 1)
    def _():
        o_ref[...] = x_ref[...] + mod_ref[0][5:6] * acc_ref[...]


def _ffn(h, wg, wu, wd, x, mods, tf, n_lat_rows, seq):
    m = h.shape[0]
    ff = wg.shape[1]
    tm = 512
    midx = _mod_index(n_lat_rows, seq, tm)
    return pl.pallas_call(
        _ffn_kernel,
        grid=(m // tm, ff // tf),
        in_specs=[pl.BlockSpec((tm, D_MODEL), lambda i, f: (i, 0)),
                  pl.BlockSpec((D_MODEL, tf), lambda i, f: (0, f)),
                  pl.BlockSpec((D_MODEL, tf), lambda i, f: (0, f)),
                  pl.BlockSpec((tf, D_MODEL), lambda i, f: (f, 0)),
                  pl.BlockSpec((tm, D_MODEL), lambda i, f: (i, 0)),
                  pl.BlockSpec((1, 6, D_MODEL), lambda i, f: (midx(i), 0, 0))],
        out_specs=pl.BlockSpec((tm, D_MODEL), lambda i, f: (i, 0)),
        out_shape=jax.ShapeDtypeStruct((m, D_MODEL), F32),
        scratch_shapes=[pltpu.VMEM((tm, D_MODEL), F32)],
        compiler_params=_cparams(("parallel", "arbitrary")),
        name="ffn",
    )(h, wg, wu, wd, x, mods)


MOE_EXTRA = 128


def _moe_kernel(cnt_ref, h_ref, gate_ref, wg_ref, wu_ref, wd_ref, x_ref, mod_ref, o_ref,
                hc_ref, y_ref, rcol_ref, rrow_ref, *, main_rows):
    i, e, f = pl.program_id(0), pl.program_id(1), pl.program_id(2)
    n_e, n_f = pl.num_programs(1), pl.num_programs(2)
    t = h_ref.shape[0]
    cnt = cnt_ref[i * N_EXPERTS + e]
    n_extra = jnp.maximum(cnt - main_rows + MOE_EXTRA - 1, 0) // MOE_EXTRA

    @pl.when(jnp.logical_and(e == 0, f == 0))
    def _():
        o_ref[...] = jnp.zeros_like(o_ref)
        ri = lax.broadcasted_iota(jnp.int32, (t, t), 0)
        ci = lax.broadcasted_iota(jnp.int32, (t, t), 1)
        before = (ci < ri).astype(BF16)
        sel = (gate_ref[...] != 0.0).astype(BF16)
        eye = (lax.broadcasted_iota(jnp.int32, (LANES, LANES), 0)
               == lax.broadcasted_iota(jnp.int32, (LANES, LANES), 1)).astype(BF16)
        sel_t = _dot_nt(eye, sel)
        rank_c = _dot(before, sel)
        rank_r = _dot_nt(_bf(sel_t), before)
        rcol_ref[...] = jnp.where(sel > 0, rank_c, -1.0)
        rrow_ref[...] = jnp.where(sel_t > 0, rank_r, -1.0)[0:N_EXPERTS]

    def compact(row0, rows):
        rr = rrow_ref[pl.ds(e, 1), :]
        want = (row0 + lax.broadcasted_iota(jnp.int32, (rows, t), 0)).astype(F32)
        hc_ref[pl.ds(row0, rows), :] = _bf(_dot((rr == want).astype(BF16), h_ref[...]))
        y_ref[pl.ds(row0, rows), :] = jnp.zeros((rows, D_MODEL), F32)

    def expert(row0, rows):
        hc = hc_ref[pl.ds(row0, rows), :]
        a = _silu(_dot(hc, wg_ref[0])) * _dot(hc, wu_ref[0])
        y_ref[pl.ds(row0, rows), :] += _dot(_bf(a), wd_ref[0])

    def scatter(row0, rows, rc, ge):
        want = (row0 + lax.broadcasted_iota(jnp.int32, (t, rows), 1)).astype(F32)
        o_ref[...] += ge * _dot((rc == want).astype(BF16), _bf(y_ref[pl.ds(row0, rows), :]))

    def extra_rows(j):
        return pl.multiple_of(main_rows + j * MOE_EXTRA, 16)

    @pl.when(f == 0)
    def _():
        compact(0, main_rows)
        lax.fori_loop(0, n_extra, lambda j, c: (compact(extra_rows(j), MOE_EXTRA), c)[1], 0)

    expert(0, main_rows)
    lax.fori_loop(0, n_extra, lambda j, c: (expert(extra_rows(j), MOE_EXTRA), c)[1], 0)

    @pl.when(f == n_f - 1)
    def _():
        lane = lax.broadcasted_iota(jnp.int32, (t, LANES), 1)
        rc = jnp.sum(jnp.where(lane == e, rcol_ref[...], 0.0), axis=-1, keepdims=True)
        ge = jnp.sum(jnp.where(lane == e, gate_ref[...], 0.0), axis=-1, keepdims=True)
        scatter(0, main_rows, rc, ge)
        lax.fori_loop(0, n_extra, lambda j, c: (scatter(extra_rows(j), MOE_EXTRA, rc, ge), c)[1], 0)

    @pl.when(jnp.logical_and(e == n_e - 1, f == n_f - 1))
    def _():
        o_ref[...] = x_ref[...] + mod_ref[0][5:6] * o_ref[...]


def _moe(h, gates, counts, wg, wu, wd, x, mods, tf, t, n_lat_rows, seq):
    m = h.shape[0]
    n_e, _, ff = wg.shape
    main_rows = 9 * t // 32
    cap = main_rows + -(-(t - main_rows) // MOE_EXTRA) * MOE_EXTRA
    midx = _mod_index(n_lat_rows, seq, t)
    grid_spec = pltpu.PrefetchScalarGridSpec(
        num_scalar_prefetch=1,
        grid=(m // t, n_e, ff // tf),
        in_specs=[pl.BlockSpec((t, D_MODEL), lambda i, e, f, c: (i, 0)),
                  pl.BlockSpec((t, LANES), lambda i, e, f, c: (i, 0)),
                  pl.BlockSpec((1, D_MODEL, tf), lambda i, e, f, c: (e, 0, f)),
                  pl.BlockSpec((1, D_MODEL, tf), lambda i, e, f, c: (e, 0, f)),
                  pl.BlockSpec((1, tf, D_MODEL), lambda i, e, f, c: (e, f, 0)),
                  pl.BlockSpec((t, D_MODEL), lambda i, e, f, c: (i, 0)),
                  pl.BlockSpec((1, 6, D_MODEL), lambda i, e, f, c: (midx(i), 0, 0))],
        out_specs=pl.BlockSpec((t, D_MODEL), lambda i, e, f, c: (i, 0)),
        scratch_shapes=[pltpu.VMEM((cap, D_MODEL), BF16), pltpu.VMEM((cap, D_MODEL), F32),
                        pltpu.VMEM((t, LANES), F32), pltpu.VMEM((N_EXPERTS, t), F32)],
    )
    return pl.pallas_call(
        functools.partial(_moe_kernel, main_rows=main_rows),
        grid_spec=grid_spec,
        out_shape=jax.ShapeDtypeStruct((m, D_MODEL), F32),
        compiler_params=_cparams(("parallel", "arbitrary", "arbitrary")),
        name="moe",
    )(counts, h, gates, wg, wu, wd, x, mods)


def _final_norm_kernel(x_ref, g_ref, o_ref):
    x = x_ref[...]
    o_ref[...] = x * lax.rsqrt(jnp.mean(x * x, axis=-1, keepdims=True) + EPS) * g_ref[...]


def _final_norm(x, gain, rows):
    tm = 512
    return pl.pallas_call(
        _final_norm_kernel,
        grid=(rows // tm,),
        in_specs=[pl.BlockSpec((tm, D_MODEL), lambda i: (i, 0)),
                  pl.BlockSpec((1, D_MODEL), lambda i: (0, 0))],
        out_specs=pl.BlockSpec((tm, D_MODEL), lambda i: (i, 0)),
        out_shape=jax.ShapeDtypeStruct((rows, D_MODEL), F32),
        compiler_params=_cparams(("parallel",)),
        name="final_norm",
    )(x, gain.reshape(1, D_MODEL))


def _split_w_in(w):
    sizes = (1536, 512, 8, 8, 1024, 512, 16, 512, 128, 128, 512, 512, 4096)
    offs = np.concatenate([[0], np.cumsum(sizes)])
    (qkv, og, a, b, xbc, z, dt, q, k, v, u, sv, gate) = [w[:, int(offs[i]):int(offs[i + 1])] for i in range(13)]
    main = jnp.concatenate([qkv, og, xbc, z, q, u, sv, gate, k, v], axis=1).astype(BF16)
    small = jnp.concatenate([a, b, dt, jnp.zeros((w.shape[0], LANES - SMALL_ROWS), w.dtype)], axis=1).astype(BF16)
    return main, small, small[:, :SMALL_ROWS].T


def _scan_params(gdn_a_log, gdn_dt_bias, ssd_a_log, ssd_dt_bias):
    zeros8 = jnp.zeros((8,), F32)
    alog = jnp.concatenate([gdn_a_log.reshape(-1), zeros8, ssd_a_log.reshape(-1)]).astype(F32)
    dtb = jnp.concatenate([gdn_dt_bias.reshape(-1), zeros8, ssd_dt_bias.reshape(-1)]).astype(F32)
    pad = jnp.zeros((LANES - SMALL_ROWS,), F32)
    pcol = jnp.zeros((8, LANES), F32).at[0].set(jnp.concatenate([alog, pad])).at[1].set(jnp.concatenate([dtb, pad]))
    prow_a = jnp.broadcast_to(alog[:, None], (SMALL_ROWS, SCAN_CHUNK))
    prow_b = jnp.broadcast_to(dtb[:, None], (SMALL_ROWS, SCAN_CHUNK))
    return pcol, prow_a, prow_b


def _rope_tables(seq, ctx_len):
    rows = seq // GRID_W
    row = jnp.repeat(jnp.arange(rows), GRID_W)
    col = jnp.tile(jnp.arange(GRID_W), rows)
    inv_freq = ROPE_BASE ** (-jnp.arange(ROPE_FREQS, dtype=F32) / ROPE_FREQS)
    ang = jnp.stack([row, col], axis=-1).astype(F32)[..., None] * inv_freq
    cos, sin = jnp.cos(ang), jnp.sin(ang)
    cos_h = jnp.concatenate([cos, cos], axis=-1).reshape(seq, ATT_HEAD_DIM)
    sin_h = jnp.concatenate([-sin, sin], axis=-1).reshape(seq, ATT_HEAD_DIM)
    cos_t = jnp.concatenate([jnp.tile(cos_h, (1, LANES // ATT_HEAD_DIM)), jnp.ones((ctx_len, LANES), F32)], axis=0)
    sin_t = jnp.concatenate([jnp.tile(sin_h, (1, LANES // ATT_HEAD_DIM)), jnp.zeros((ctx_len, LANES), F32)], axis=0)
    return cos_t, sin_t


def kernel(x, c, ctx, c_ctx, w_ada, b_ada, norm1, norm2, w_in, gdn_conv, gdn_A_log, gdn_dt_bias, gdn_norm,
           ssd_conv, ssd_conv_b, ssd_A_log, ssd_dt_bias, ssd_D, ssd_norm, attn_sink, mlp_ws, mlp_bs,
           w_branch, w_out, ffn_wg, ffn_wu, ffn_wd, moe_router, moe_wg, moe_wu, moe_wd, final_norm):
    bsz, seq, _ = x.shape
    ctx_len = ctx.shape[1]
    depth = w_ada.shape[0]
    n_lat = bsz * seq
    assert seq % 512 == 0 and ctx_len == CONV_TILE and (bsz * ctx_len) % 512 == 0

    xf = jnp.concatenate([x.reshape(n_lat, D_MODEL), ctx.reshape(bsz * ctx_len, D_MODEL)], axis=0)
    c_rows = jnp.zeros((16, D_MODEL), F32).at[0].set(c_ctx).at[1:1 + bsz].set(c)
    mods_all = _ada_mods(c_rows, w_ada, b_ada).reshape(depth, 16, 6, D_MODEL)
    cos_t, sin_t = _rope_tables(seq, ctx_len)
    moe_block = math.gcd(1024, seq, bsz * ctx_len)

    for i in range(depth):
        mods = mods_all[i]
        w_main, w_small, w_small_t = _split_w_in(w_in[i])
        pcol, prow_a, prow_b = _scan_params(gdn_A_log[i], gdn_dt_bias[i], ssd_A_log[i], ssd_dt_bias[i])

        h = _normmod(xf, norm1[i], mods, 0, n_lat, seq, BF16)
        proj = _matmul(h, w_main, BF16, 512, PROJ_MAIN // 2)
        small, small_t = _proj_small(h, w_small, w_small_t)
        small_t = small_t.reshape(SMALL_ROWS, -1, SCAN_CHUNK).transpose(1, 0, 2)

        qkv = _conv_act(proj, C_QKV, 3 * GDN_WIDTH, gdn_conv[i], jnp.zeros((3 * GDN_WIDTH,), F32), n_lat, seq, True)
        xbc = _conv_act(proj, C_XBC, SSD_CONV_CH, ssd_conv[i], ssd_conv_b[i], n_lat, seq, False)
        gdn_f, gdn_b = _gdn_mixer(qkv, small, small_t, pcol, prow_a, prow_b, bsz, seq, ctx_len)
        ssd_f, ssd_b = _ssd_scan(xbc, small, small_t, pcol, prow_a, prow_b, bsz, seq, ctx_len)
        att = _attention(proj, attn_sink[i].astype(F32), cos_t, sin_t, bsz, seq, ctx_len)
        bs_b = jnp.broadcast_to(mlp_bs[i][:, :, None], (MLP_GROUPS, MLP_CHUNK, MLP_GROUP_DIM)).astype(F32)
        mlp = _gmlp(proj, mlp_ws[i].astype(BF16), bs_b)

        xf = _merge(gdn_f, gdn_b, ssd_f, ssd_b, xbc, att, mlp, proj, xf, mods,
                    w_branch[i].astype(BF16), w_out[i].astype(BF16),
                    jnp.tile(gdn_norm[i], GDN_HEADS).reshape(1, GDN_WIDTH).astype(F32),
                    jnp.repeat(ssd_D[i], SSD_HEAD_DIM).reshape(1, SSD_WIDTH).astype(F32),
                    ssd_norm[i].reshape(1, SSD_WIDTH).astype(F32), n_lat, seq)

        j = i // 2
        if i % 2 == 0:
            h2 = _normmod(xf, norm2[i], mods, 3, n_lat, seq, BF16)
            xf = _ffn(h2, ffn_wg[j].astype(BF16), ffn_wu[j].astype(BF16), ffn_wd[j].astype(BF16),
                      xf, mods, 1408, n_lat, seq)
        else:
            router = jnp.concatenate([moe_router[j], jnp.zeros((D_MODEL, LANES - N_EXPERTS), F32)], axis=1)
            h2, gates, cnt = _normmod(xf, norm2[i], mods, 3, n_lat, seq, BF16, router=router)
            cnt = cnt[:, 0, :N_EXPERTS].reshape(-1, moe_block // ROUTER_TILE, N_EXPERTS).sum(axis=1)
            xf = _moe(h2, gates, cnt.astype(jnp.int32).reshape(-1), moe_wg[j].astype(BF16),
                      moe_wu[j].astype(BF16), moe_wd[j].astype(BF16), xf, mods, 896, moe_block, n_lat, seq)

    return _final_norm(xf, final_norm, n_lat).reshape(bsz, seq, D_MODEL)
```

```python
import functools
import math

import jax
import jax.numpy as jnp
import numpy as np
from jax import lax
from jax.experimental import pallas as pl
from jax.experimental.pallas import tpu as pltpu

F32 = jnp.float32
BF16 = jnp.bfloat16

D_MODEL = 1024
GRID_W = 64
EPS = 1e-6
N_DIR = 2

GDN_HEADS = 4
GDN_HEAD_DIM = 128
GDN_WIDTH = 512
SSD_HEADS = 8
SSD_HEAD_DIM = 64
SSD_WIDTH = 512
SSD_GROUPS = 2
SSD_STATE = 128
SSD_CONV_CH = 1024
ATT_HEADS = 8
ATT_KV_HEADS = 2
ATT_HEAD_DIM = 64
ATT_WIDTH = 512
ATT_KV_WIDTH = 128
WINDOW = 128
ATT_BLOCK = 128
ROPE_BASE = 10000.0
ROPE_FREQS = 16
MLP_GROUPS = 4
MLP_GROUP_DIM = 128
MLP_WIDTH = 512
MLP_CHUNK = 128
N_BRANCH = 4
BRANCH_WIDTH = 512
N_EXPERTS = 8
TOP_K = 2

SCAN_CHUNK = 64
LANES = 128
VMEM_LIMIT = 56 * 1024 * 1024

C_QKV, C_OG, C_XBC, C_Z, C_Q, C_U, C_SV, C_GATE, C_K, C_V = (
    0, 1536, 2048, 3072, 3584, 4096, 4608, 5120, 9216, 9344)
PROJ_MAIN = 9472
S_A, S_B, S_DT = 0, 8, 16
SMALL_ROWS = 32


def _cparams(sem):
    return pltpu.CompilerParams(dimension_semantics=sem, vmem_limit_bytes=VMEM_LIMIT)


def _bf(x):
    return x.astype(BF16)


def _dot(a, b):
    return jnp.dot(a, b, preferred_element_type=F32)


def _dot_nt(a, b):
    return lax.dot_general(a, b, (((1,), (1,)), ((), ())), preferred_element_type=F32)


def _dot_tn(a, b):
    return lax.dot_general(a, b, (((0,), (0,)), ((), ())), preferred_element_type=F32)


def _split(a):
    hi = a.astype(BF16)
    lo = (a - hi.astype(F32)).astype(BF16)
    return hi, lo


def _dot3(a, b, dot=_dot):
    ah, al = _split(a)
    bh, bl = _split(b)
    return dot(ah, bh) + (dot(al, bh) + dot(ah, bl))


def _silu(x):
    return x * (1.0 / (1.0 + jnp.exp(-x)))


def _sigmoid(x):
    return 1.0 / (1.0 + jnp.exp(-x))


def _softplus(x):
    return jnp.maximum(x, 0.0) + jnp.log1p(jnp.exp(-jnp.abs(x)))


def _gelu_tanh(x):
    return 0.5 * x * (1.0 + jnp.tanh(math.sqrt(2.0 / math.pi) * (x + 0.044715 * (x * x * x))))


def _ada_kernel(c_ref, w_ref, b_ref, o_ref):
    a = _silu(c_ref[...])
    o_ref[0] = _dot3(a, w_ref[0]) + b_ref[0]


def _ada_mods(c_rows, w_ada, b_ada):
    depth, _, n = w_ada.shape
    tn = 1536
    return pl.pallas_call(
        _ada_kernel,
        grid=(depth, n // tn),
        in_specs=[pl.BlockSpec((16, D_MODEL), lambda l, j: (0, 0)),
                  pl.BlockSpec((1, D_MODEL, tn), lambda l, j: (l, 0, j)),
                  pl.BlockSpec((1, 1, tn), lambda l, j: (l, 0, j))],
        out_specs=pl.BlockSpec((1, 16, tn), lambda l, j: (l, 0, j)),
        out_shape=jax.ShapeDtypeStruct((depth, 16, n), F32),
        compiler_params=_cparams(("parallel", "parallel")),
        name="ada_mods",
    )(c_rows, w_ada, b_ada.reshape(depth, 1, n))


def _normmod_kernel(x_ref, g_ref, mod_ref, h_ref, *, row0):
    x = x_ref[...]
    xn = x * lax.rsqrt(jnp.mean(x * x, axis=-1, keepdims=True) + EPS) * g_ref[...]
    mod = mod_ref[0]
    h = xn * (1.0 + mod[row0 + 1:row0 + 2]) + mod[row0:row0 + 1]
    h_ref[...] = h.astype(h_ref.dtype)


def _normmod_router_kernel(x_ref, g_ref, mod_ref, r_ref, h_ref, gate_ref, cnt_ref, *, row0):
    x = x_ref[...]
    xn = x * lax.rsqrt(jnp.mean(x * x, axis=-1, keepdims=True) + EPS) * g_ref[...]
    mod = mod_ref[0]
    h = xn * (1.0 + mod[row0 + 1:row0 + 2]) + mod[row0:row0 + 1]
    h_ref[...] = h.astype(h_ref.dtype)
    logits = _dot3(h, r_ref[...])
    lane = lax.broadcasted_iota(jnp.int32, logits.shape, 1).astype(F32)
    logits = jnp.where(lane < N_EXPERTS, logits, -jnp.inf)
    m1 = jnp.max(logits, axis=-1, keepdims=True)
    i1 = jnp.min(jnp.where(logits == m1, lane, float(LANES)), axis=-1, keepdims=True)
    rest = jnp.where(lane == i1, -jnp.inf, logits)
    m2 = jnp.max(rest, axis=-1, keepdims=True)
    i2 = jnp.min(jnp.where(rest == m2, lane, float(LANES)), axis=-1, keepdims=True)
    e = jnp.exp(m2 - m1)
    w1 = 1.0 / (1.0 + e)
    w2 = e / (1.0 + e)
    gate = jnp.where(lane == i1, w1, 0.0) + jnp.where(lane == i2, w2, 0.0)
    gate_ref[...] = gate
    cnt = jnp.sum((gate != 0.0).astype(F32), axis=0, keepdims=True)
    cnt_ref[0] = jnp.broadcast_to(cnt, (8, LANES))


def _mod_index(n_lat_rows, seq, tm):
    def idx(i):
        r = i * tm
        return jnp.where(r < n_lat_rows, 1 + r // seq, 0)
    return idx


ROUTER_TILE = 512


def _normmod(x, gain, mods, row0, n_lat_rows, seq, out_dtype, router=None):
    m = x.shape[0]
    tm = ROUTER_TILE
    midx = _mod_index(n_lat_rows, seq, tm)
    in_specs = [pl.BlockSpec((tm, D_MODEL), lambda i: (i, 0)),
                pl.BlockSpec((1, D_MODEL), lambda i: (0, 0)),
                pl.BlockSpec((1, 6, D_MODEL), lambda i: (midx(i), 0, 0))]
    if router is None:
        return pl.pallas_call(
            functools.partial(_normmod_kernel, row0=row0),
            grid=(m // tm,),
            in_specs=in_specs,
            out_specs=pl.BlockSpec((tm, D_MODEL), lambda i: (i, 0)),
            out_shape=jax.ShapeDtypeStruct((m, D_MODEL), out_dtype),
            compiler_params=_cparams(("parallel",)),
            name="normmod",
        )(x, gain.reshape(1, D_MODEL), mods)
    return pl.pallas_call(
        functools.partial(_normmod_router_kernel, row0=row0),
        grid=(m // tm,),
        in_specs=in_specs + [pl.BlockSpec((D_MODEL, LANES), lambda i: (0, 0))],
        out_specs=[pl.BlockSpec((tm, D_MODEL), lambda i: (i, 0)),
                   pl.BlockSpec((tm, LANES), lambda i: (i, 0)),
                   pl.BlockSpec((1, 8, LANES), lambda i: (i, 0, 0))],
        out_shape=[jax.ShapeDtypeStruct((m, D_MODEL), out_dtype),
                   jax.ShapeDtypeStruct((m, LANES), F32),
                   jax.ShapeDtypeStruct((m // tm, 8, LANES), F32)],
        compiler_params=_cparams(("parallel",)),
        name="normmod_router",
    )(x, gain.reshape(1, D_MODEL), mods, router)


def _mm_kernel(a_ref, b_ref, o_ref):
    o_ref[...] = _dot(a_ref[...], b_ref[...]).astype(o_ref.dtype)


def _matmul(a, b, out_dtype, tm, tn):
    m, k = a.shape
    n = b.shape[1]
    return pl.pallas_call(
        _mm_kernel,
        grid=(n // tn, m // tm),
        in_specs=[pl.BlockSpec((tm, k), lambda j, i: (i, 0)),
                  pl.BlockSpec((k, tn), lambda j, i: (0, j))],
        out_specs=pl.BlockSpec((tm, tn), lambda j, i: (i, j)),
        out_shape=jax.ShapeDtypeStruct((m, n), out_dtype),
        compiler_params=_cparams(("parallel", "parallel")),
        name="proj_main",
    )(a, b)


def _proj_small_kernel(a_ref, w_ref, wt_ref, o_ref, ot_ref):
    a = a_ref[...]
    o_ref[...] = _dot(a, w_ref[...])
    ot_ref[...] = _dot_nt(wt_ref[...], a)


def _proj_small(h, w_small, w_small_t):
    m = h.shape[0]
    tm = 512
    return pl.pallas_call(
        _proj_small_kernel,
        grid=(m // tm,),
        in_specs=[pl.BlockSpec((tm, D_MODEL), lambda i: (i, 0)),
                  pl.BlockSpec((D_MODEL, LANES), lambda i: (0, 0)),
                  pl.BlockSpec((SMALL_ROWS, D_MODEL), lambda i: (0, 0))],
        out_specs=[pl.BlockSpec((tm, LANES), lambda i: (i, 0)),
                   pl.BlockSpec((SMALL_ROWS, tm), lambda i: (0, i))],
        out_shape=[jax.ShapeDtypeStruct((m, LANES), F32),
                   jax.ShapeDtypeStruct((SMALL_ROWS, m), F32)],
        compiler_params=_cparams(("parallel",)),
        name="proj_small",
    )(h, w_small, w_small_t)


CONV_TILE = 256
HALO = 16


def _conv_kernel(x_ref, prev_ref, next_ref, w_ref, b_ref, o_ref, *, tiles_per_seq, n_lat_tiles, l2_heads):
    j = pl.program_id(0)
    is_lat = j < n_lat_tiles
    first = jnp.logical_or(jnp.logical_not(is_lat), j % tiles_per_seq == 0)
    last = jnp.logical_or(jnp.logical_not(is_lat), j % tiles_per_seq == tiles_per_seq - 1)
    x = x_ref[...].astype(F32)
    rows = x.shape[0]
    prow = jnp.where(first, 0.0, prev_ref[...].astype(F32)[HALO - 1:HALO])
    nrow = jnp.where(last, 0.0, next_ref[...].astype(F32)[0:1])
    ridx = lax.broadcasted_iota(jnp.int32, x.shape, 0)
    xp = jnp.where(ridx == 0, prow, pltpu.roll(x, 1, axis=0))
    xn = jnp.where(ridx == rows - 1, nrow, pltpu.roll(x, rows - 1, axis=0))
    w = w_ref[...]
    y = _silu(w[0:1] * xp + w[1:2] * x + w[2:3] * xn + b_ref[...])
    if l2_heads:
        parts = []
        for hh in range(2 * GDN_HEADS):
            seg = y[:, hh * GDN_HEAD_DIM:(hh + 1) * GDN_HEAD_DIM]
            inv = lax.rsqrt(jnp.sum(seg * seg, axis=-1, keepdims=True) + EPS)
            if hh < GDN_HEADS:
                inv = inv * (GDN_HEAD_DIM ** -0.5)
            parts.append(seg * inv)
        parts.append(y[:, 2 * GDN_WIDTH:])
        y = jnp.concatenate(parts, axis=1)
    o_ref[...] = y.astype(o_ref.dtype)


def _conv_act(proj, col0, width, w, b, n_lat_rows, seq, l2_heads):
    m = proj.shape[0]
    n_tiles = m // CONV_TILE
    per_tile = CONV_TILE // HALO
    n_halo = m // HALO
    cb = col0 // width
    assert col0 % width == 0
    return pl.pallas_call(
        functools.partial(_conv_kernel, tiles_per_seq=seq // CONV_TILE,
                          n_lat_tiles=n_lat_rows // CONV_TILE, l2_heads=l2_heads),
        grid=(n_tiles,),
        in_specs=[pl.BlockSpec((CONV_TILE, width), lambda j: (j, cb)),
                  pl.BlockSpec((HALO, width), lambda j: (jnp.maximum(j * per_tile - 1, 0), cb)),
                  pl.BlockSpec((HALO, width), lambda j: (jnp.minimum((j + 1) * per_tile, n_halo - 1), cb)),
                  pl.BlockSpec((3, width), lambda j: (0, 0)),
                  pl.BlockSpec((1, width), lambda j: (0, 0))],
        out_specs=pl.BlockSpec((CONV_TILE, width), lambda j: (j, 0)),
        out_shape=jax.ShapeDtypeStruct((m, width), BF16),
        compiler_params=_cparams(("parallel",)),
        name="conv_act",
    )(proj, proj, proj, w, b.reshape(1, width))


def _chunk_maps(bsz, seq, ctx_len, cs):
    nc_ctx, nc_lat = ctx_len // cs, seq // cs
    lat_blocks = bsz * nc_lat

    def fwd(b, t):
        return jnp.where(t < nc_ctx, lat_blocks + b * nc_ctx + t, b * nc_lat + (t - nc_ctx))

    def bwd(b, t):
        return jnp.where(t < nc_ctx, lat_blocks + b * nc_ctx + (nc_ctx - 1 - t),
                         b * nc_lat + (nc_lat - 1 - (t - nc_ctx)))

    return fwd, bwd, nc_ctx + nc_lat


def _tri_masks(c):
    ri = lax.broadcasted_iota(jnp.int32, (c, c), 0)
    ci = lax.broadcasted_iota(jnp.int32, (c, c), 1)
    return (ri >= ci, ri <= ci), (ri > ci, ri < ci), ri == ci


def _bdot(a, b):
    return lax.dot_general(a, b, (((2,), (1,)), ((0,), (0,))), preferred_element_type=F32)


def _bdot1(a, b):
    return _bdot(_bf(a), _bf(b))


def _unit_tri_solve(nmat, rhs):
    c = nmat.shape[1]
    ri = lax.broadcasted_iota(jnp.int32, (c, c), 0)
    ci = lax.broadcasted_iota(jnp.int32, (c, c), 1)
    same = lambda size: ((ri // size) == (ci // size))[None]
    base = 16
    m = jnp.where(same(base), nmat, 0.0)
    t = (ri == ci).astype(F32)[None] - m
    k = 2
    while k < base:
        m = _bdot1(m, m)
        t = t + _bdot1(t, m)
        k *= 2
    size = base
    while size < c:
        off = jnp.where(jnp.logical_and(same(2 * size), jnp.logical_not(same(size))), nmat, 0.0)
        t = t - _bdot1(_bdot1(t, off), t)
        size *= 2
    x0 = _bdot1(t, rhs)
    nh, nl = _split(nmat)
    xh, xl = _split(x0)
    resid = rhs - x0 - (_bdot(nh, xh) + (_bdot(nl, xh) + _bdot(nh, xl)))
    return x0 + _bdot1(t, resid)


GDN_LOCAL_CHUNKS = 2


def _gdn_local_kernel(qkv_ref, sm_ref, smt_ref, pcol_ref, prow_a_ref, prow_b_ref,
                      a1_ref, u_ref, a2_ref, eg_ref):
    c = SCAN_CHUNK
    incl, strict, _ = _tri_masks(c)
    tri = incl[0].astype(F32)
    eye_k = (lax.broadcasted_iota(jnp.int32, (GDN_HEAD_DIM, GDN_HEAD_DIM), 0)
             == lax.broadcasted_iota(jnp.int32, (GDN_HEAD_DIM, GDN_HEAD_DIM), 1)).astype(BF16)
    pcol = pcol_ref[...]
    alog_row, dtb_row = pcol[0:1], pcol[1:2]
    nmats, rhss, rest = [], [], []
    for ck in range(GDN_LOCAL_CHUNKS):
        rows = slice(ck * c, (ck + 1) * c)
        sm = sm_ref[rows, :]
        smt = smt_ref[ck]
        g_cols = -jnp.exp(alog_row) * _softplus(sm + dtb_row)
        beta_cols = _sigmoid(sm)
        g_rows = -jnp.exp(prow_a_ref[...]) * _softplus(smt + prow_b_ref[...])
        pre_cols = _dot3(tri, g_cols)
        pre_rows = _dot3(g_rows, tri, dot=_dot_nt)
        tot_cols, tot_rows = pre_cols[c - 1:c, :], pre_rows[:, c - 1:c]
        gcum_cols = (pre_cols, tot_cols - pre_cols + g_cols)
        gcum_rows = (pre_rows, tot_rows - pre_rows + g_rows)
        for hh in range(GDN_HEADS):
            q = qkv_ref[rows, hh * GDN_HEAD_DIM:(hh + 1) * GDN_HEAD_DIM]
            k = qkv_ref[rows, GDN_WIDTH + hh * GDN_HEAD_DIM:GDN_WIDTH + (hh + 1) * GDN_HEAD_DIM]
            v = qkv_ref[rows, 2 * GDN_WIDTH + hh * GDN_HEAD_DIM:2 * GDN_WIDTH + (hh + 1) * GDN_HEAD_DIM]
            kk = _dot_nt(k, k)
            qk = _dot_nt(q, k)
            kt = _dot_nt(eye_k, k)
            qf, kf, vf = q.astype(F32), k.astype(F32), v.astype(F32)
            for d in range(N_DIR):
                col = S_A + d * GDN_HEADS + hh
                gc = gcum_cols[d][:, col:col + 1]
                gr = gcum_rows[d][col:col + 1, :]
                g_last = tot_rows[col:col + 1, :]
                decay = jnp.where(incl[d], jnp.exp(jnp.minimum(gc - gr, 0.0)), 0.0)
                beta = beta_cols[:, S_B + d * GDN_HEADS + hh:S_B + d * GDN_HEADS + hh + 1]
                eg = jnp.exp(gc)
                nmats.append(jnp.where(strict[d], kk * decay, 0.0) * beta)
                rhss.append(jnp.concatenate([vf * beta, kf * (beta * eg)], axis=1))
                rest.append((d, ck, hh, qf * eg, qk * decay, kt * jnp.exp(g_last - gr), jnp.exp(g_last)))
    sol = _unit_tri_solve(jnp.stack(nmats), jnp.stack(rhss))
    for idx, (d, ck, hh, q_in, qkm, kout_t, eg_last) in enumerate(rest):
        u_ref[d, ck, hh] = _bf(sol[idx, :, :GDN_HEAD_DIM])
        a1_ref[d, ck, hh, 0:c, :] = _bf(sol[idx, :, GDN_HEAD_DIM:])
        a1_ref[d, ck, hh, c:2 * c, :] = _bf(q_in)
        a2_ref[d, ck, hh, 0:c, :] = _bf(qkm)
        a2_ref[d, ck, hh, c:c + GDN_HEAD_DIM, :] = _bf(kout_t)
        row = d * GDN_HEADS + hh
        eg_ref[ck, row:row + 1, :] = jnp.broadcast_to(eg_last, (1, LANES))


def _gdn_scan_kernel(a1f_ref, a1b_ref, uf_ref, ub_ref, a2f_ref, a2b_ref, egf_ref, egb_ref,
                     of_ref, ob_ref, state_ref):
    t = pl.program_id(1)

    @pl.when(t == 0)
    def _():
        state_ref[...] = jnp.zeros_like(state_ref)

    c = SCAN_CHUNK
    dirs = ((a1f_ref, uf_ref, a2f_ref, egf_ref, of_ref), (a1b_ref, ub_ref, a2b_ref, egb_ref, ob_ref))
    chains = [(d, hh) for d in range(N_DIR) for hh in range(GDN_HEADS)]
    states = {ch: state_ref[ch[0], ch[1]] for ch in chains}
    m1 = {(d, hh): _dot(dirs[d][0][0, 0, hh], _bf(states[d, hh])) for d, hh in chains}
    m2 = {(d, hh): _dot(dirs[d][2][0, 0, hh], _bf(dirs[d][1][0, 0, hh].astype(F32) - m1[d, hh][0:c]))
          for d, hh in chains}
    for d, hh in chains:
        row = d * GDN_HEADS + hh
        dirs[d][4][:, hh * GDN_HEAD_DIM:(hh + 1) * GDN_HEAD_DIM] = m1[d, hh][c:2 * c] + m2[d, hh][0:c]
        state_ref[d, hh] = states[d, hh] * dirs[d][3][0, row:row + 1, :] + m2[d, hh][c:c + GDN_HEAD_DIM]


def _gdn_mixer(qkv, small, small_t, pcol, prow_a, prow_b, bsz, seq, ctx_len):
    m = qkv.shape[0]
    c = SCAN_CHUNK
    n_chunks = m // c
    cb = GDN_LOCAL_CHUNKS
    hd = GDN_HEAD_DIM
    const = lambda shape: pl.BlockSpec(shape, lambda i: (0,) * len(shape))
    a1, u, a2, eg = pl.pallas_call(
        _gdn_local_kernel,
        grid=(n_chunks // cb,),
        in_specs=[pl.BlockSpec((cb * c, 3 * GDN_WIDTH), lambda i: (i, 0)),
                  pl.BlockSpec((cb * c, LANES), lambda i: (i, 0)),
                  pl.BlockSpec((cb, SMALL_ROWS, c), lambda i: (i, 0, 0)),
                  const((8, LANES)), const((SMALL_ROWS, c)), const((SMALL_ROWS, c))],
        out_specs=[pl.BlockSpec((N_DIR, cb, GDN_HEADS, 2 * c, hd), lambda i: (0, i, 0, 0, 0)),
                   pl.BlockSpec((N_DIR, cb, GDN_HEADS, c, hd), lambda i: (0, i, 0, 0, 0)),
                   pl.BlockSpec((N_DIR, cb, GDN_HEADS, c + hd, c), lambda i: (0, i, 0, 0, 0)),
                   pl.BlockSpec((cb, 8, LANES), lambda i: (i, 0, 0))],
        out_shape=[jax.ShapeDtypeStruct((N_DIR, n_chunks, GDN_HEADS, 2 * c, hd), BF16),
                   jax.ShapeDtypeStruct((N_DIR, n_chunks, GDN_HEADS, c, hd), BF16),
                   jax.ShapeDtypeStruct((N_DIR, n_chunks, GDN_HEADS, c + hd, c), BF16),
                   jax.ShapeDtypeStruct((n_chunks, 8, LANES), F32)],
        compiler_params=_cparams(("parallel",)),
        name="gdn_local",
    )(qkv, small, small_t, pcol, prow_a, prow_b)

    fwd, bwd, steps = _chunk_maps(bsz, seq, ctx_len, c)
    per_dir = lambda rows, cols: [pl.BlockSpec((1, 1, GDN_HEADS, rows, cols), lambda b, t: (0, fwd(b, t), 0, 0, 0)),
                                  pl.BlockSpec((1, 1, GDN_HEADS, rows, cols), lambda b, t: (1, bwd(b, t), 0, 0, 0))]
    out = lambda f: pl.BlockSpec((c, GDN_WIDTH), lambda b, t: (f(b, t), 0))
    return pl.pallas_call(
        _gdn_scan_kernel,
        grid=(bsz, steps),
        in_specs=(per_dir(2 * c, hd) + per_dir(c, hd) + per_dir(c + hd, c)
                  + [pl.BlockSpec((1, 8, LANES), lambda b, t: (fwd(b, t), 0, 0)),
                     pl.BlockSpec((1, 8, LANES), lambda b, t: (bwd(b, t), 0, 0))]),
        out_specs=[out(fwd), out(bwd)],
        out_shape=[jax.ShapeDtypeStruct((m, GDN_WIDTH), F32)] * 2,
        scratch_shapes=[pltpu.VMEM((N_DIR, GDN_HEADS, hd, hd), F32)],
        compiler_params=_cparams(("parallel", "arbitrary")),
        name="gdn_scan",
    )(a1, a1, u, u, a2, a2, eg, eg)


def _lane_expand(cols, width):
    rows = cols[0].shape[0]
    lane = lax.broadcasted_iota(jnp.int32, (rows, len(cols) * width), 1) // width
    out = jnp.broadcast_to(cols[-1], lane.shape)
    for r in range(len(cols) - 2, -1, -1):
        out = jnp.where(lane == r, cols[r], out)
    return out


def _ssd_kernel(xbc_f_ref, xbc_b_ref, sm_f_ref, sm_b_ref, smt_f_ref, smt_b_ref,
                pcol_ref, prow_a_ref, prow_b_ref, yf_ref, yb_ref, state_ref):
    t = pl.program_id(1)

    @pl.when(t == 0)
    def _():
        state_ref[...] = jnp.zeros_like(state_ref)

    c = SCAN_CHUNK
    p = SSD_HEAD_DIM
    hpg = SSD_HEADS // SSD_GROUPS
    gw = hpg * p
    incl, _, _ = _tri_masks(c)
    tri = incl[0].astype(F32)
    eye_n = (lax.broadcasted_iota(jnp.int32, (SSD_STATE, SSD_STATE), 0)
             == lax.broadcasted_iota(jnp.int32, (SSD_STATE, SSD_STATE), 1)).astype(BF16)
    pcol = pcol_ref[...]
    alog_row, dtb_row = pcol[0:1], pcol[1:2]
    dirs = ((xbc_f_ref, sm_f_ref, smt_f_ref, yf_ref), (xbc_b_ref, sm_b_ref, smt_b_ref, yb_ref))
    lmats, xdts, per = [], [], []
    for d, (xbc_ref, sm_ref, smt_ref, y_ref) in enumerate(dirs):
        dt_cols = _softplus(sm_ref[...] + dtb_row)
        a_cols = -jnp.exp(alog_row) * dt_cols
        a_rows = -jnp.exp(prow_a_ref[...]) * _softplus(smt_ref[0] + prow_b_ref[...])
        pre_cols = _dot3(tri, a_cols)
        pre_rows = _dot3(a_rows, tri, dot=_dot_nt)
        tot_cols, tot_rows = pre_cols[c - 1:c, :], pre_rows[:, c - 1:c]
        if d == 0:
            acs_cols, acs_rows = pre_cols, pre_rows
        else:
            acs_cols, acs_rows = tot_cols - pre_cols + a_cols, tot_rows - pre_rows + a_rows
        for g in range(SSD_GROUPS):
            bm = xbc_ref[:, SSD_WIDTH + g * SSD_STATE:SSD_WIDTH + (g + 1) * SSD_STATE]
            cm = xbc_ref[:, SSD_WIDTH + (SSD_GROUPS + g) * SSD_STATE:SSD_WIDTH + (SSD_GROUPS + g + 1) * SSD_STATE]
            cb = _dot_nt(cm, bm)
            bm_t = _bf(_dot_nt(eye_n, bm))
            cols = [S_DT + d * SSD_HEADS + g * hpg + r for r in range(hpg)]
            acs = [acs_cols[:, col:col + 1] for col in cols]
            a_last = [tot_rows[col:col + 1, :] for col in cols]
            xdt = xbc_ref[:, g * gw:(g + 1) * gw].astype(F32) * _lane_expand(
                [dt_cols[:, col:col + 1] for col in cols], p)
            for r, col in enumerate(cols):
                ar = acs_rows[col:col + 1, :]
                lmats.append(cb * jnp.where(incl[d], jnp.exp(jnp.minimum(acs[r] - ar, 0.0)), 0.0))
                xdts.append(xdt[:, r * p:(r + 1) * p])
            decay_out = _lane_expand([jnp.exp(al - ac) for al, ac in zip(a_last, acs)], p)
            cs = _dot(bm_t, _bf(xdt * decay_out))
            s = state_ref[d, g]
            y_off = _dot(cm, _bf(s)) * _lane_expand([jnp.exp(ac) for ac in acs], p)
            s_new = s * _lane_expand([jnp.exp(al) for al in a_last], p) + cs
            per.append((d, g, y_ref, y_off, s_new))
    y_diag = _bdot1(jnp.stack(lmats), jnp.stack(xdts))
    for idx, (d, g, y_ref, y_off, s_new) in enumerate(per):
        for r in range(hpg):
            lo = g * gw + r * p
            y_ref[:, lo:lo + p] = y_diag[idx * hpg + r] + y_off[:, r * p:(r + 1) * p]
        state_ref[d, g] = s_new


def _ssd_scan(xbc, small, small_t, pcol, prow_a, prow_b, bsz, seq, ctx_len):
    m = xbc.shape[0]
    c = SCAN_CHUNK
    fwd, bwd, steps = _chunk_maps(bsz, seq, ctx_len, c)
    spec = lambda width, f: pl.BlockSpec((c, width), lambda b, t: (f(b, t), 0))
    spec_t = lambda f: pl.BlockSpec((1, SMALL_ROWS, c), lambda b, t: (f(b, t), 0, 0))
    const = lambda shape: pl.BlockSpec(shape, lambda b, t: (0,) * len(shape))
    return pl.pallas_call(
        _ssd_kernel,
        grid=(bsz, steps),
        in_specs=[spec(SSD_CONV_CH, fwd), spec(SSD_CONV_CH, bwd), spec(LANES, fwd), spec(LANES, bwd),
                  spec_t(fwd), spec_t(bwd), const((8, LANES)), const((SMALL_ROWS, c)), const((SMALL_ROWS, c))],
        out_specs=[spec(SSD_WIDTH, fwd), spec(SSD_WIDTH, bwd)],
        out_shape=[jax.ShapeDtypeStruct((m, SSD_WIDTH), F32)] * 2,
        scratch_shapes=[pltpu.VMEM((N_DIR, SSD_GROUPS, SSD_STATE, SSD_WIDTH // SSD_GROUPS), F32)],
        compiler_params=_cparams(("parallel", "arbitrary")),
        name="ssd_scan",
    )(xbc, xbc, small, small, small_t, small_t, pcol, prow_a, prow_b)


def _rope(t, cos, sin_signed):
    width = t.shape[1]
    lane = lax.broadcasted_iota(jnp.int32, t.shape, 1)
    partner = jnp.where(lane % 32 < ROPE_FREQS, pltpu.roll(t, width - ROPE_FREQS, axis=1),
                        pltpu.roll(t, ROPE_FREQS, axis=1))
    return t * cos + partner * sin_signed


def _attn_kernel(sink_ref, q_ref, k0_ref, k1_ref, k2_ref, v0_ref, v1_ref, v2_ref, kc_ref, vc_ref,
                 cq_ref, sq_ref, c0_ref, c1_ref, c2_ref, s0_ref, s1_ref, s2_ref, o_ref, *, n_lat_blocks):
    i = pl.program_id(1)
    blk = ATT_BLOCK
    is_lat = i < n_lat_blocks
    rep = ATT_HEADS // ATT_KV_HEADS
    cq = jnp.concatenate([cq_ref[...]] * (ATT_WIDTH // LANES), axis=1)
    sq = jnp.concatenate([sq_ref[...]] * (ATT_WIDTH // LANES), axis=1)
    q = _rope(q_ref[...].astype(F32), cq, sq) * (ATT_HEAD_DIM ** -0.5)
    kwin = jnp.concatenate([_rope(k0_ref[...].astype(F32), c0_ref[...], s0_ref[...]),
                            _rope(k1_ref[...].astype(F32), c1_ref[...], s1_ref[...]),
                            _rope(k2_ref[...].astype(F32), c2_ref[...], s2_ref[...])], axis=0)
    vwin = jnp.concatenate([v0_ref[...], v1_ref[...], v2_ref[...]], axis=0)
    kc = kc_ref[...]
    vc = vc_ref[...]
    rows = rep * blk
    qpos = i * blk + lax.broadcasted_iota(jnp.int32, (rows, 3 * blk), 0) % blk
    kpos = (i - 1) * blk + lax.broadcasted_iota(jnp.int32, (rows, 3 * blk), 1)
    valid = (jnp.abs(qpos - kpos) <= WINDOW) & (kpos >= 0) & (kpos < n_lat_blocks * blk) & is_lat
    rgrp = lax.broadcasted_iota(jnp.int32, (rows, 1), 0) // blk
    for g in range(ATT_KV_HEADS):
        qg = q[:, g * rep * ATT_HEAD_DIM:(g + 1) * rep * ATT_HEAD_DIM]
        qs = _bf(jnp.concatenate([qg[:, r * ATT_HEAD_DIM:(r + 1) * ATT_HEAD_DIM] for r in range(rep)], axis=0))
        kg = _bf(kwin[:, g * ATT_HEAD_DIM:(g + 1) * ATT_HEAD_DIM])
        vg = vwin[:, g * ATT_HEAD_DIM:(g + 1) * ATT_HEAD_DIM]
        s_loc = jnp.where(valid, _dot_nt(qs, kg), -jnp.inf)
        s_ctx = _dot_nt(qs, kc[:, g * ATT_HEAD_DIM:(g + 1) * ATT_HEAD_DIM])
        sink = jnp.zeros((rows, 1), F32)
        for r in range(rep):
            sink = jnp.where(rgrp == r, sink_ref[g * rep + r], sink)
        mx = jnp.maximum(jnp.maximum(jnp.max(s_loc, axis=-1, keepdims=True),
                                     jnp.max(s_ctx, axis=-1, keepdims=True)), sink)
        p_loc = jnp.exp(s_loc - mx)
        p_ctx = jnp.exp(s_ctx - mx)
        den = (jnp.sum(p_loc, axis=-1, keepdims=True) + jnp.sum(p_ctx, axis=-1, keepdims=True)
               + jnp.exp(sink - mx))
        o = (_dot(_bf(p_loc), vg) + _dot(_bf(p_ctx), vc[:, g * ATT_HEAD_DIM:(g + 1) * ATT_HEAD_DIM])) / den
        for r in range(rep):
            hcol = (g * rep + r) * ATT_HEAD_DIM
            o_ref[:, hcol:hcol + ATT_HEAD_DIM] = o[r * blk:(r + 1) * blk].astype(o_ref.dtype)


def _attention(proj, sink, cos_t, sin_t, bsz, seq, ctx_len):
    m = proj.shape[0]
    blk = ATT_BLOCK
    nl, ncx = seq // blk, ctx_len // blk
    lat_blocks = bsz * nl

    def qrow(b, i):
        return jnp.where(i < nl, b * nl + i, lat_blocks + b * ncx + (i - nl))

    def krow(off):
        return lambda b, i, s: (b * nl + jnp.clip(i + off, 0, nl - 1), C_K // ATT_KV_WIDTH)

    def vrow(off):
        return lambda b, i, s: (b * nl + jnp.clip(i + off, 0, nl - 1), C_V // ATT_KV_WIDTH)

    def trow(off):
        return lambda b, i, s: (jnp.clip(i + off, 0, nl - 1), 0)

    kv_spec = lambda f: pl.BlockSpec((blk, ATT_KV_WIDTH), f)
    tab = lambda f: pl.BlockSpec((blk, LANES), f)
    ctx_blk = lambda colblk: pl.BlockSpec((ctx_len, ATT_KV_WIDTH),
                                          lambda b, i, s: (bsz * seq // ctx_len + b, colblk))
    grid_spec = pltpu.PrefetchScalarGridSpec(
        num_scalar_prefetch=1,
        grid=(bsz, nl + ncx),
        in_specs=[pl.BlockSpec((blk, ATT_WIDTH), lambda b, i, s: (qrow(b, i), C_Q // ATT_WIDTH)),
                  kv_spec(krow(-1)), kv_spec(krow(0)), kv_spec(krow(1)),
                  kv_spec(vrow(-1)), kv_spec(vrow(0)), kv_spec(vrow(1)),
                  ctx_blk(C_K // ATT_KV_WIDTH), ctx_blk(C_V // ATT_KV_WIDTH),
                  tab(lambda b, i, s: (i, 0)), tab(lambda b, i, s: (i, 0)),
                  tab(trow(-1)), tab(trow(0)), tab(trow(1)),
                  tab(trow(-1)), tab(trow(0)), tab(trow(1))],
        out_specs=pl.BlockSpec((blk, ATT_WIDTH), lambda b, i, s: (qrow(b, i), 0)),
    )
    return pl.pallas_call(
        functools.partial(_attn_kernel, n_lat_blocks=nl),
        grid_spec=grid_spec,
        out_shape=jax.ShapeDtypeStruct((m, ATT_WIDTH), BF16),
        compiler_params=_cparams(("parallel", "parallel")),
        name="window_attn",
    )(sink, proj, proj, proj, proj, proj, proj, proj, proj, proj,
      cos_t, sin_t, cos_t, cos_t, cos_t, sin_t, sin_t, sin_t)


GMLP_CHUNKS = 2


def _gmlp_kernel(u_ref, sv_ref, ws_ref, bs_ref, o_ref):
    for ck in range(GMLP_CHUNKS):
        rows = slice(ck * MLP_CHUNK, (ck + 1) * MLP_CHUNK)
        u = _gelu_tanh(u_ref[rows, :].astype(F32))
        v = _gelu_tanh(sv_ref[rows, :].astype(F32))
        for g in range(MLP_GROUPS):
            vg = v[:, g * MLP_GROUP_DIM:(g + 1) * MLP_GROUP_DIM]
            mu = jnp.mean(vg, axis=-1, keepdims=True)
            vc = vg - mu
            vn = vc * lax.rsqrt(jnp.mean(vc * vc, axis=-1, keepdims=True) + EPS)
            sp = _dot(ws_ref[g], _bf(vn)) + bs_ref[g]
            o_ref[rows, g * MLP_GROUP_DIM:(g + 1) * MLP_GROUP_DIM] = (
                u[:, g * MLP_GROUP_DIM:(g + 1) * MLP_GROUP_DIM] * sp).astype(o_ref.dtype)


def _gmlp(proj, ws, bs_b):
    m = proj.shape[0]
    ck = MLP_CHUNK
    tm = GMLP_CHUNKS * ck
    return pl.pallas_call(
        _gmlp_kernel,
        grid=(m // tm,),
        in_specs=[pl.BlockSpec((tm, MLP_WIDTH), lambda i: (i, C_U // MLP_WIDTH)),
                  pl.BlockSpec((tm, MLP_WIDTH), lambda i: (i, C_SV // MLP_WIDTH)),
                  pl.BlockSpec((MLP_GROUPS, ck, ck), lambda i: (0, 0, 0)),
                  pl.BlockSpec((MLP_GROUPS, ck, MLP_GROUP_DIM), lambda i: (0, 0, 0))],
        out_specs=pl.BlockSpec((tm, MLP_WIDTH), lambda i: (i, 0)),
        out_shape=jax.ShapeDtypeStruct((m, MLP_WIDTH), BF16),
        compiler_params=_cparams(("parallel",)),
        name="gmlp",
    )(proj, proj, ws, bs_b)


def _group_rms(y, gain, group):
    parts = []
    for s in range(0, y.shape[1], group):
        seg = y[:, s:s + group]
        parts.append(seg * lax.rsqrt(jnp.mean(seg * seg, axis=-1, keepdims=True) + EPS))
    return jnp.concatenate(parts, axis=1) * gain


def _merge_kernel(gf_ref, gb_ref, og_ref, sf_ref, sb_ref, xc_ref, z_ref, att_ref, mlp_ref,
                  g0_ref, g1_ref, g2_ref, g3_ref, x_ref, mod_ref, wb_ref, wo_ref,
                  gn_ref, dsk_ref, sn_ref, o_ref):
    ya = _group_rms(gf_ref[...] + gb_ref[...], gn_ref[...], GDN_HEAD_DIM) * _silu(og_ref[...].astype(F32))
    yb = sf_ref[...] + sb_ref[...] + dsk_ref[...] * xc_ref[...].astype(F32)
    yb = _group_rms(yb * _silu(z_ref[...].astype(F32)), sn_ref[...], SSD_WIDTH // SSD_GROUPS)
    ys = (_bf(ya), _bf(yb), att_ref[...], mlp_ref[...])
    acc = None
    for mi, (y, g_ref) in enumerate(zip(ys, (g0_ref, g1_ref, g2_ref, g3_ref))):
        term = _sigmoid(g_ref[...].astype(F32)) * _dot(y, wb_ref[mi])
        acc = term if acc is None else acc + term
    out = _dot(_bf(acc), wo_ref[...])
    o_ref[...] = x_ref[...] + mod_ref[0][2:3] * out


def _merge(gdn_f, gdn_b, ssd_f, ssd_b, xbc, att, mlp, proj, x, mods, wb, wo, gn, dsk, sn, n_lat_rows, seq):
    m = x.shape[0]
    tm = 256
    midx = _mod_index(n_lat_rows, seq, tm)
    half = lambda cb=0: pl.BlockSpec((tm, BRANCH_WIDTH), lambda i: (i, cb))
    gate = lambda k: pl.BlockSpec((tm, D_MODEL), lambda i: (i, C_GATE // D_MODEL + k))
    row = lambda w: pl.BlockSpec((1, w), lambda i: (0, 0))
    return pl.pallas_call(
        _merge_kernel,
        grid=(m // tm,),
        in_specs=[half(), half(), half(C_OG // BRANCH_WIDTH), half(), half(), half(0), half(C_Z // BRANCH_WIDTH),
                  half(), half(), gate(0), gate(1), gate(2), gate(3),
                  pl.BlockSpec((tm, D_MODEL), lambda i: (i, 0)),
                  pl.BlockSpec((1, 6, D_MODEL), lambda i: (midx(i), 0, 0)),
                  pl.BlockSpec((N_BRANCH, BRANCH_WIDTH, D_MODEL), lambda i: (0, 0, 0)),
                  pl.BlockSpec((D_MODEL, D_MODEL), lambda i: (0, 0)),
                  row(BRANCH_WIDTH), row(BRANCH_WIDTH), row(BRANCH_WIDTH)],
        out_specs=pl.BlockSpec((tm, D_MODEL), lambda i: (i, 0)),
        out_shape=jax.ShapeDtypeStruct((m, D_MODEL), F32),
        compiler_params=_cparams(("parallel",)),
        name="merge",
    )(gdn_f, gdn_b, proj, ssd_f, ssd_b, xbc, proj, att, mlp, proj, proj, proj, proj, x, mods, wb, wo, gn, dsk, sn)


def _ffn_kernel(h_ref, wg_ref, wu_ref, wd_ref, x_ref, mod_ref, o_ref, acc_ref):
    f = pl.program_id(1)

    @pl.when(f == 0)
    def _():
        acc_ref[...] = jnp.zeros_like(acc_ref)

    h = h_ref[...]
    a = _silu(_dot(h, wg_ref[...])) * _dot(h, wu_ref[...])
    acc_ref[...] += _dot(_bf(a), wd_ref[...])

    @pl.when(f == pl.num_programs(1) - 1)
    def _():
        o_ref[...] = x_ref[...] + mod_ref[0][5:6] * acc_ref[...]


def _ffn(h, wg, wu, wd, x, mods, tf, n_lat_rows, seq):
    m = h.shape[0]
    ff = wg.shape[1]
    tm = 512
    midx = _mod_index(n_lat_rows, seq, tm)
    return pl.pallas_call(
        _ffn_kernel,
        grid=(m // tm, ff // tf),
        in_specs=[pl.BlockSpec((tm, D_MODEL), lambda i, f: (i, 0)),
                  pl.BlockSpec((D_MODEL, tf), lambda i, f: (0, f)),
                  pl.BlockSpec((D_MODEL, tf), lambda i, f: (0, f)),
                  pl.BlockSpec((tf, D_MODEL), lambda i, f: (f, 0)),
                  pl.BlockSpec((tm, D_MODEL), lambda i, f: (i, 0)),
                  pl.BlockSpec((1, 6, D_MODEL), lambda i, f: (midx(i), 0, 0))],
        out_specs=pl.BlockSpec((tm, D_MODEL), lambda i, f: (i, 0)),
        out_shape=jax.ShapeDtypeStruct((m, D_MODEL), F32),
        scratch_shapes=[pltpu.VMEM((tm, D_MODEL), F32)],
        compiler_params=_cparams(("parallel", "arbitrary")),
        name="ffn",
    )(h, wg, wu, wd, x, mods)


MOE_EXTRA = 128


def _moe_kernel(cnt_ref, h_ref, gate_ref, wg_ref, wu_ref, wd_ref, x_ref, mod_ref, o_ref,
                hc_ref, y_ref, rcol_ref, rrow_ref, *, main_rows):
    i, e, f = pl.program_id(0), pl.program_id(1), pl.program_id(2)
    n_e, n_f = pl.num_programs(1), pl.num_programs(2)
    t = h_ref.shape[0]
    cnt = cnt_ref[i * N_EXPERTS + e]
    n_extra = jnp.maximum(cnt - main_rows + MOE_EXTRA - 1, 0) // MOE_EXTRA

    @pl.when(jnp.logical_and(e == 0, f == 0))
    def _():
        o_ref[...] = jnp.zeros_like(o_ref)
        ri = lax.broadcasted_iota(jnp.int32, (t, t), 0)
        ci = lax.broadcasted_iota(jnp.int32, (t, t), 1)
        before = (ci < ri).astype(BF16)
        sel = (gate_ref[...] != 0.0).astype(BF16)
        eye = (lax.broadcasted_iota(jnp.int32, (LANES, LANES), 0)
               == lax.broadcasted_iota(jnp.int32, (LANES, LANES), 1)).astype(BF16)
        sel_t = _dot_nt(eye, sel)
        rank_c = _dot(before, sel)
        rank_r = _dot_nt(_bf(sel_t), before)
        rcol_ref[...] = jnp.where(sel > 0, rank_c, -1.0)
        rrow_ref[...] = jnp.where(sel_t > 0, rank_r, -1.0)[0:N_EXPERTS]

    def compact(row0, rows):
        rr = rrow_ref[pl.ds(e, 1), :]
        want = (row0 + lax.broadcasted_iota(jnp.int32, (rows, t), 0)).astype(F32)
        hc_ref[pl.ds(row0, rows), :] = _bf(_dot((rr == want).astype(BF16), h_ref[...]))
        y_ref[pl.ds(row0, rows), :] = jnp.zeros((rows, D_MODEL), F32)

    def expert(row0, rows):
        hc = hc_ref[pl.ds(row0, rows), :]
        a = _silu(_dot(hc, wg_ref[0, 0])) * _dot(hc, wu_ref[0, 0])
        y_ref[pl.ds(row0, rows), :] += _dot(_bf(a), wd_ref[0])

    def scatter(row0, rows, rc, ge):
        want = (row0 + lax.broadcasted_iota(jnp.int32, (t, rows), 1)).astype(F32)
        o_ref[...] += ge * _dot((rc == want).astype(BF16), _bf(y_ref[pl.ds(row0, rows), :]))

    def extra_rows(j):
        return pl.multiple_of(main_rows + j * MOE_EXTRA, 16)

    @pl.when(f == 0)
    def _():
        compact(0, main_rows)
        lax.fori_loop(0, n_extra, lambda j, c: (compact(extra_rows(j), MOE_EXTRA), c)[1], 0)

    expert(0, main_rows)
    lax.fori_loop(0, n_extra, lambda j, c: (expert(extra_rows(j), MOE_EXTRA), c)[1], 0)

    @pl.when(f == n_f - 1)
    def _():
        lane = lax.broadcasted_iota(jnp.int32, (t, LANES), 1)
        rc = jnp.sum(jnp.where(lane == e, rcol_ref[...], 0.0), axis=-1, keepdims=True)
        ge = jnp.sum(jnp.where(lane == e, gate_ref[...], 0.0), axis=-1, keepdims=True)
        scatter(0, main_rows, rc, ge)
        lax.fori_loop(0, n_extra, lambda j, c: (scatter(extra_rows(j), MOE_EXTRA, rc, ge), c)[1], 0)

    @pl.when(jnp.logical_and(e == n_e - 1, f == n_f - 1))
    def _():
        o_ref[...] = x_ref[...] + mod_ref[0][5:6] * o_ref[...]


def _moe(h, gates, counts, wg, wu, wd, x, mods, tf, t, n_lat_rows, seq):
    m = h.shape[0]
    n_e, n_f = wg.shape[:2]
    main_rows = 9 * t // 32
    cap = main_rows + -(-(t - main_rows) // MOE_EXTRA) * MOE_EXTRA
    midx = _mod_index(n_lat_rows, seq, t)
    grid_spec = pltpu.PrefetchScalarGridSpec(
        num_scalar_prefetch=1,
        grid=(m // t, n_e, n_f),
        in_specs=[pl.BlockSpec((t, D_MODEL), lambda i, e, f, c: (i, 0)),
                  pl.BlockSpec((t, LANES), lambda i, e, f, c: (i, 0)),
                  pl.BlockSpec((1, 1, D_MODEL, tf), lambda i, e, f, c: (e, f, 0, 0)),
                  pl.BlockSpec((1, 1, D_MODEL, tf), lambda i, e, f, c: (e, f, 0, 0)),
                  pl.BlockSpec((1, tf, D_MODEL), lambda i, e, f, c: (e, f, 0)),
                  pl.BlockSpec((t, D_MODEL), lambda i, e, f, c: (i, 0)),
                  pl.BlockSpec((1, 6, D_MODEL), lambda i, e, f, c: (midx(i), 0, 0))],
        out_specs=pl.BlockSpec((t, D_MODEL), lambda i, e, f, c: (i, 0)),
        scratch_shapes=[pltpu.VMEM((cap, D_MODEL), BF16), pltpu.VMEM((cap, D_MODEL), F32),
                        pltpu.VMEM((t, LANES), F32), pltpu.VMEM((N_EXPERTS, t), F32)],
    )
    return pl.pallas_call(
        functools.partial(_moe_kernel, main_rows=main_rows),
        grid_spec=grid_spec,
        out_shape=jax.ShapeDtypeStruct((m, D_MODEL), F32),
        compiler_params=_cparams(("parallel", "arbitrary", "arbitrary")),
        name="moe",
    )(counts, h, gates, wg, wu, wd, x, mods)


def _final_norm_kernel(x_ref, g_ref, o_ref):
    x = x_ref[...]
    o_ref[...] = x * lax.rsqrt(jnp.mean(x * x, axis=-1, keepdims=True) + EPS) * g_ref[...]


def _final_norm(x, gain, rows):
    tm = 512
    return pl.pallas_call(
        _final_norm_kernel,
        grid=(rows // tm,),
        in_specs=[pl.BlockSpec((tm, D_MODEL), lambda i: (i, 0)),
                  pl.BlockSpec((1, D_MODEL), lambda i: (0, 0))],
        out_specs=pl.BlockSpec((tm, D_MODEL), lambda i: (i, 0)),
        out_shape=jax.ShapeDtypeStruct((rows, D_MODEL), F32),
        compiler_params=_cparams(("parallel",)),
        name="final_norm",
    )(x, gain.reshape(1, D_MODEL))


def _split_w_in(w):
    sizes = (1536, 512, 8, 8, 1024, 512, 16, 512, 128, 128, 512, 512, 4096)
    offs = np.concatenate([[0], np.cumsum(sizes)])
    (qkv, og, a, b, xbc, z, dt, q, k, v, u, sv, gate) = [w[:, int(offs[i]):int(offs[i + 1])] for i in range(13)]
    main = jnp.concatenate([qkv, og, xbc, z, q, u, sv, gate, k, v], axis=1).astype(BF16)
    small = jnp.concatenate([a, b, dt, jnp.zeros((w.shape[0], LANES - SMALL_ROWS), w.dtype)], axis=1).astype(BF16)
    return main, small, small[:, :SMALL_ROWS].T


def _scan_params(gdn_a_log, gdn_dt_bias, ssd_a_log, ssd_dt_bias):
    zeros8 = jnp.zeros((8,), F32)
    alog = jnp.concatenate([gdn_a_log.reshape(-1), zeros8, ssd_a_log.reshape(-1)]).astype(F32)
    dtb = jnp.concatenate([gdn_dt_bias.reshape(-1), zeros8, ssd_dt_bias.reshape(-1)]).astype(F32)
    pad = jnp.zeros((LANES - SMALL_ROWS,), F32)
    pcol = jnp.zeros((8, LANES), F32).at[0].set(jnp.concatenate([alog, pad])).at[1].set(jnp.concatenate([dtb, pad]))
    prow_a = jnp.broadcast_to(alog[:, None], (SMALL_ROWS, SCAN_CHUNK))
    prow_b = jnp.broadcast_to(dtb[:, None], (SMALL_ROWS, SCAN_CHUNK))
    return pcol, prow_a, prow_b


def _rope_tables(seq, ctx_len):
    rows = seq // GRID_W
    row = jnp.repeat(jnp.arange(rows), GRID_W)
    col = jnp.tile(jnp.arange(GRID_W), rows)
    inv_freq = ROPE_BASE ** (-jnp.arange(ROPE_FREQS, dtype=F32) / ROPE_FREQS)
    ang = jnp.stack([row, col], axis=-1).astype(F32)[..., None] * inv_freq
    cos, sin = jnp.cos(ang), jnp.sin(ang)
    cos_h = jnp.concatenate([cos, cos], axis=-1).reshape(seq, ATT_HEAD_DIM)
    sin_h = jnp.concatenate([-sin, sin], axis=-1).reshape(seq, ATT_HEAD_DIM)
    cos_t = jnp.concatenate([jnp.tile(cos_h, (1, LANES // ATT_HEAD_DIM)), jnp.ones((ctx_len, LANES), F32)], axis=0)
    sin_t = jnp.concatenate([jnp.tile(sin_h, (1, LANES // ATT_HEAD_DIM)), jnp.zeros((ctx_len, LANES), F32)], axis=0)
    return cos_t, sin_t


def kernel(x, c, ctx, c_ctx, w_ada, b_ada, norm1, norm2, w_in, gdn_conv, gdn_A_log, gdn_dt_bias, gdn_norm,
           ssd_conv, ssd_conv_b, ssd_A_log, ssd_dt_bias, ssd_D, ssd_norm, attn_sink, mlp_ws, mlp_bs,
           w_branch, w_out, ffn_wg, ffn_wu, ffn_wd, moe_router, moe_wg, moe_wu, moe_wd, final_norm):
    bsz, seq, _ = x.shape
    ctx_len = ctx.shape[1]
    depth = w_ada.shape[0]
    n_lat = bsz * seq
    assert seq % 512 == 0 and ctx_len == CONV_TILE and (bsz * ctx_len) % 512 == 0

    xf = jnp.concatenate([x.reshape(n_lat, D_MODEL), ctx.reshape(bsz * ctx_len, D_MODEL)], axis=0)
    c_rows = jnp.zeros((16, D_MODEL), F32).at[0].set(c_ctx).at[1:1 + bsz].set(c)
    mods_all = _ada_mods(c_rows, w_ada, b_ada).reshape(depth, 16, 6, D_MODEL)
    cos_t, sin_t = _rope_tables(seq, ctx_len)
    moe_block = math.gcd(1024, seq, bsz * ctx_len)

    for i in range(depth):
        mods = mods_all[i]
        w_main, w_small, w_small_t = _split_w_in(w_in[i])
        pcol, prow_a, prow_b = _scan_params(gdn_A_log[i], gdn_dt_bias[i], ssd_A_log[i], ssd_dt_bias[i])

        h = _normmod(xf, norm1[i], mods, 0, n_lat, seq, BF16)
        proj = _matmul(h, w_main, BF16, 512, PROJ_MAIN // 2)
        small, small_t = _proj_small(h, w_small, w_small_t)
        small_t = small_t.reshape(SMALL_ROWS, -1, SCAN_CHUNK).transpose(1, 0, 2)

        qkv = _conv_act(proj, C_QKV, 3 * GDN_WIDTH, gdn_conv[i], jnp.zeros((3 * GDN_WIDTH,), F32), n_lat, seq, True)
        xbc = _conv_act(proj, C_XBC, SSD_CONV_CH, ssd_conv[i], ssd_conv_b[i], n_lat, seq, False)
        gdn_f, gdn_b = _gdn_mixer(qkv, small, small_t, pcol, prow_a, prow_b, bsz, seq, ctx_len)
        ssd_f, ssd_b = _ssd_scan(xbc, small, small_t, pcol, prow_a, prow_b, bsz, seq, ctx_len)
        att = _attention(proj, attn_sink[i].astype(F32), cos_t, sin_t, bsz, seq, ctx_len)
        bs_b = jnp.broadcast_to(mlp_bs[i][:, :, None], (MLP_GROUPS, MLP_CHUNK, MLP_GROUP_DIM)).astype(F32)
        mlp = _gmlp(proj, mlp_ws[i].astype(BF16), bs_b)

        xf = _merge(gdn_f, gdn_b, ssd_f, ssd_b, xbc, att, mlp, proj, xf, mods,
                    w_branch[i].astype(BF16), w_out[i].astype(BF16),
                    jnp.tile(gdn_norm[i], GDN_HEADS).reshape(1, GDN_WIDTH).astype(F32),
                    jnp.repeat(ssd_D[i], SSD_HEAD_DIM).reshape(1, SSD_WIDTH).astype(F32),
                    ssd_norm[i].reshape(1, SSD_WIDTH).astype(F32), n_lat, seq)

        j = i // 2
        if i % 2 == 0:
            h2 = _normmod(xf, norm2[i], mods, 3, n_lat, seq, BF16)
            xf = _ffn(h2, ffn_wg[j].astype(BF16), ffn_wu[j].astype(BF16), ffn_wd[j].astype(BF16),
                      xf, mods, 1408, n_lat, seq)
        else:
            router = jnp.concatenate([moe_router[j], jnp.zeros((D_MODEL, LANES - N_EXPERTS), F32)], axis=1)
            h2, gates, cnt = _normmod(xf, norm2[i], mods, 3, n_lat, seq, BF16, router=router)
            cnt = cnt[:, 0, :N_EXPERTS].reshape(-1, moe_block // ROUTER_TILE, N_EXPERTS).sum(axis=1)
            tf = 896
            slices = lambda w: w.astype(BF16).reshape(N_EXPERTS, D_MODEL, -1, tf).transpose(0, 2, 1, 3)
            xf = _moe(h2, gates, cnt.astype(jnp.int32).reshape(-1), slices(moe_wg[j]), slices(moe_wu[j]),
                      moe_wd[j].astype(BF16), xf, mods, tf, moe_block, n_lat, seq)

    return _final_norm(xf, final_norm, n_lat).reshape(bsz, seq, D_MODEL)
```
